```python
import math
import jax
import jax.numpy as jnp
from jax import lax
import numpy as np


D_MODEL = 2048
BATCH = 8
SEQ = 4096
DEPTH = 4

N_BRANCH = 3
N_MOD = 6
EPS = 1e-6
GMLP_GROUPS = 8
GMLP_GROUP_DIM = 128
GMLP_WIDTH = GMLP_GROUPS * GMLP_GROUP_DIM
GMLP_CHUNK = 128
DN_HEADS = 8
DN_HEAD_DIM = 128
DN_WIDTH = DN_HEADS * DN_HEAD_DIM
DN_CONV = 4
DN_CHUNK = 64
MLA_HEADS = 8
MLA_Q_RANK = 512
MLA_KV_RANK = 512
MLA_NOPE_DIM = 128
MLA_ROPE_DIM = 64
MLA_V_DIM = 128
MLA_WIDTH = MLA_HEADS * MLA_V_DIM
ATTN_BLOCK = 128
ROPE_THETA = 10000.0
N_EXPERTS = 16
N_EXPERT_GROUPS = 4
EXPERTS_PER_GROUP = N_EXPERTS // N_EXPERT_GROUPS
TOP_K = 2
D_EXPERT = 1024
MOE_BLOCK = 256
IN_WIDTHS = (GMLP_WIDTH, GMLP_WIDTH, 3 * DN_WIDTH, DN_WIDTH, DN_HEADS, DN_HEADS, MLA_Q_RANK, MLA_KV_RANK, MLA_ROPE_DIM, N_BRANCH * D_MODEL)
IN_COLS = sum(IN_WIDTHS)

kernel_name = 'hybrid_gmlp_deltanet_mla_grouped_moe_adaln'


def rmsnorm(x, g):
    xf = x.astype(jnp.float32)
    y = xf * lax.rsqrt(jnp.mean(xf * xf, axis=-1, keepdims=True) + EPS)
    return (y * g).astype(x.dtype)


def layernorm(x, g, b):
    xf = x.astype(jnp.float32)
    mu = jnp.mean(xf, axis=-1, keepdims=True)
    var = jnp.mean(jnp.square(xf - mu), axis=-1, keepdims=True)
    return ((xf - mu) * lax.rsqrt(var + EPS) * g + b).astype(x.dtype)


def l2norm(x):
    return x * lax.rsqrt(jnp.sum(x * x, axis=-1, keepdims=True) + EPS)


def rope_tables(positions):
    inv_freq = ROPE_THETA ** (-jnp.arange(0, MLA_ROPE_DIM, 2, dtype=jnp.float32) / MLA_ROPE_DIM)
    ang = positions.astype(jnp.float32)[..., None] * inv_freq
    return jnp.cos(ang), jnp.sin(ang)


def apply_rope(x, cos, sin):
    x1, x2 = jnp.split(x.astype(jnp.float32), 2, axis=-1)
    return jnp.concatenate([x1 * cos - x2 * sin, x2 * cos + x1 * sin], axis=-1).astype(x.dtype)


def gmlp_branch(u, v, ln_g, ln_b, w_s, b_s):
    B, T, _ = v.shape
    n_chunks = T // GMLP_CHUNK
    v = layernorm(v, ln_g, ln_b).reshape(B, n_chunks, GMLP_CHUNK, GMLP_GROUPS, GMLP_GROUP_DIM)
    causal = jnp.tril(jnp.ones((GMLP_CHUNK, GMLP_CHUNK), dtype=bool))
    w = jnp.where(causal, w_s, 0.0)
    s = jnp.einsum('gij,bnjgd->bnigd', w, v) + b_s.T[None, None, :, :, None]
    return u * s.reshape(B, T, GMLP_WIDTH)


def causal_depthwise_conv(x, w):
    k, ch = w.shape
    return lax.conv_general_dilated(x, w[:, None, :], window_strides=(1,), padding=[(k - 1, 0)],
                                    dimension_numbers=('NWC', 'WIO', 'NWC'), feature_group_count=ch)


def chunked_gated_delta_rule(q, k, v, g, beta):
    out_dtype = v.dtype
    q, k, v, g, beta = (t.astype(jnp.float32) for t in (q, k, v, g, beta))
    B, T, H, Dk = q.shape
    Dv = v.shape[-1]
    C = DN_CHUNK
    N = T // C
    to_chunks = lambda t: t.reshape(B, N, C, H, t.shape[-1]).transpose(1, 0, 3, 2, 4)
    qc = to_chunks(q * Dk ** -0.5)
    kc = to_chunks(k)
    vc = to_chunks(v)
    gc = jnp.cumsum(g.reshape(B, N, C, H).transpose(1, 0, 3, 2), axis=-1)
    bc = beta.reshape(B, N, C, H).transpose(1, 0, 3, 2)
    incl = jnp.tril(jnp.ones((C, C), dtype=bool))
    strict = jnp.tril(jnp.ones((C, C), dtype=bool), k=-1)
    decay = jnp.exp(jnp.where(incl, gc[..., :, None] - gc[..., None, :], -jnp.inf))
    kb = kc * bc[..., None]
    lower = jnp.where(strict, jnp.einsum('nbhid,nbhjd->nbhij', kb, kc) * decay, 0.0)
    eye = jnp.eye(C, dtype=jnp.float32)
    wy = lax.linalg.triangular_solve(eye + lower, jnp.broadcast_to(eye, lower.shape),
                                     left_side=True, lower=True, unit_diagonal=True)
    u_c = jnp.einsum('nbhij,nbhjd->nbhid', wy, vc * bc[..., None])
    w_c = jnp.einsum('nbhij,nbhjd->nbhid', wy, kb * jnp.exp(gc)[..., None])
    intra = jnp.where(incl, jnp.einsum('nbhid,nbhjd->nbhij', qc, kc) * decay, 0.0)

    def step(state, xs):
        q_i, k_i, u_i, w_i, a_i, g_i = xs
        v_new = u_i - jnp.einsum('bhcd,bhde->bhce', w_i, state)
        o_i = (jnp.einsum('bhcd,bhde->bhce', q_i * jnp.exp(g_i)[..., None], state)
               + jnp.einsum('bhij,bhje->bhie', a_i, v_new))
        g_last = g_i[..., -1]
        k_dec = k_i * jnp.exp(g_last[..., None] - g_i)[..., None]
        state = state * jnp.exp(g_last)[..., None, None] + jnp.einsum('bhcd,bhce->bhde', k_dec, v_new)
        return state, o_i

    s0 = jnp.zeros((B, H, Dk, Dv), jnp.float32)
    _, o = lax.scan(step, s0, (qc, kc, u_c, w_c, intra, gc))
    return o.transpose(1, 0, 3, 2, 4).reshape(B, T, H, Dv).astype(out_dtype)


def deltanet_branch(qkv, z, b, a, conv_w, a_log, dt_bias, norm_g):
    B, T, _ = qkv.shape
    qkv = jax.nn.silu(causal_depthwise_conv(qkv, conv_w))
    q, k, v = (t.reshape(B, T, DN_HEADS, DN_HEAD_DIM) for t in jnp.split(qkv, 3, axis=-1))
    q = l2norm(q.astype(jnp.float32))
    k = l2norm(k.astype(jnp.float32))
    beta = jax.nn.sigmoid(b.astype(jnp.float32))
    g = -jnp.exp(a_log.astype(jnp.float32)) * jax.nn.softplus(a.astype(jnp.float32) + dt_bias)
    o = chunked_gated_delta_rule(q, k, v, g, beta)
    o = rmsnorm(o, norm_g) * jax.nn.silu(z.reshape(B, T, DN_HEADS, DN_HEAD_DIM))
    return o.reshape(B, T, DN_WIDTH)


def mla_branch(c_q, c_kv, k_rope, positions, cq_g, ckv_g, w_uq, w_ukv):
    B, T, _ = c_q.shape
    q = (rmsnorm(c_q, cq_g) @ w_uq).reshape(B, T, MLA_HEADS, MLA_NOPE_DIM + MLA_ROPE_DIM)
    kv = (rmsnorm(c_kv, ckv_g) @ w_ukv).reshape(B, T, MLA_HEADS, MLA_NOPE_DIM + MLA_V_DIM)
    cos, sin = rope_tables(positions)
    q_nope = q[..., :MLA_NOPE_DIM]
    q_rope = apply_rope(q[..., MLA_NOPE_DIM:], cos[:, :, None, :], sin[:, :, None, :])
    k_nope, v = kv[..., :MLA_NOPE_DIM], kv[..., MLA_NOPE_DIM:]
    k_rope = apply_rope(k_rope, cos, sin)
    scale = (MLA_NOPE_DIM + MLA_ROPE_DIM) ** -0.5
    n_blocks = T // ATTN_BLOCK
    to_blocks = lambda t: t.reshape(B, n_blocks, ATTN_BLOCK, *t.shape[2:]).swapaxes(0, 1)
    key_idx = jnp.arange(T)

    def attend(xs):
        qn, qr, start = xs
        s = jnp.einsum('bqhd,bkhd->bhqk', qn, k_nope) + jnp.einsum('bqhr,bkr->bhqk', qr, k_rope)
        q_idx = start + jnp.arange(ATTN_BLOCK)
        mask = key_idx[None, :] <= q_idx[:, None]
        p = jax.nn.softmax(jnp.where(mask, s.astype(jnp.float32) * scale, -jnp.inf), axis=-1)
        return jnp.einsum('bhqk,bkhd->bqhd', p.astype(v.dtype), v)

    o = lax.map(attend, (to_blocks(q_nope), to_blocks(q_rope), jnp.arange(n_blocks) * ATTN_BLOCK))
    return o.swapaxes(0, 1).reshape(B, T, MLA_WIDTH)


def token_mixer(h, positions, w_in, gmlp_ln_g, gmlp_ln_b, gmlp_ws, gmlp_bs, dn_conv_w, dn_a_log,
                dn_dt_bias, dn_norm_g, mla_cq_g, mla_ckv_g, mla_w_uq, mla_w_ukv,
                w_br_gmlp, w_br_dn, w_br_mla, w_out):
    B, T, D = h.shape
    splits = [int(s) for s in np.cumsum(IN_WIDTHS)[:-1]]
    u, v, qkv, z, b, a, c_q, c_kv, k_rope, gates = jnp.split(h @ w_in, splits, axis=-1)
    y_a = gmlp_branch(jax.nn.gelu(u), jax.nn.gelu(v), gmlp_ln_g, gmlp_ln_b, gmlp_ws, gmlp_bs)
    y_b = deltanet_branch(qkv, z, b, a, dn_conv_w, dn_a_log, dn_dt_bias, dn_norm_g)
    y_c = mla_branch(c_q, c_kv, k_rope, positions, mla_cq_g, mla_ckv_g, mla_w_uq, mla_w_ukv)
    gate = jax.nn.sigmoid(gates.astype(jnp.float32)).astype(h.dtype).reshape(B, T, N_BRANCH, D)
    merged = (gate[:, :, 0] * (y_a @ w_br_gmlp) + gate[:, :, 1] * (y_b @ w_br_dn)
              + gate[:, :, 2] * (y_c @ w_br_mla))
    return merged @ w_out


def route(h, w_router, router_bias):
    n = h.shape[0]
    scores = jax.nn.sigmoid((h @ w_router).astype(jnp.float32))
    biased = (scores + router_bias.astype(jnp.float32)).reshape(n, N_EXPERT_GROUPS, EXPERTS_PER_GROUP)
    group_score = jnp.sum(lax.top_k(biased, 2)[0], axis=-1)
    best = jnp.argmax(group_score, axis=-1)
    in_group = biased[jnp.arange(n), best]
    _, local = lax.top_k(in_group, TOP_K)
    idx = best[:, None] * EXPERTS_PER_GROUP + local
    w = jnp.take_along_axis(scores, idx, axis=1)
    return idx, w / jnp.sum(w, axis=-1, keepdims=True)


def moe_ffn(h, idx, wts, w_gate, w_up, w_down):
    n, d = h.shape
    n_assign = n * TOP_K
    flat_e = idx.reshape(n_assign)
    order = jnp.argsort(flat_e)
    sorted_e = flat_e[order]
    counts = jnp.bincount(flat_e, length=N_EXPERTS)
    padded = (counts + MOE_BLOCK - 1) // MOE_BLOCK * MOE_BLOCK
    pad_end = jnp.cumsum(padded)
    pad_start = pad_end - padded
    start = jnp.cumsum(counts) - counts
    dest = pad_start[sorted_e] + jnp.arange(n_assign) - start[sorted_e]
    n_blocks = -(-n_assign // MOE_BLOCK) + N_EXPERTS
    src_tok = order // TOP_K
    row_tok = jnp.full((n_blocks * MOE_BLOCK,), n, jnp.int32).at[dest].set(src_tok.astype(jnp.int32))
    h_pad = jnp.concatenate([h, jnp.zeros((1, d), h.dtype)], axis=0)
    xb = h_pad[row_tok].reshape(n_blocks, MOE_BLOCK, d)
    blk_e = jnp.minimum(jnp.searchsorted(pad_end, jnp.arange(n_blocks) * MOE_BLOCK, side='right'), N_EXPERTS - 1)

    def expert_block(xs):
        xblk, e = xs
        return (jax.nn.silu(xblk @ w_gate[e]) * (xblk @ w_up[e])) @ w_down[e]

    yb = lax.map(expert_block, (xb, blk_e)).reshape(n_blocks * MOE_BLOCK, d)
    y_sorted = yb[dest]
    wt_sorted = wts.reshape(n_assign)[order]
    out = jax.ops.segment_sum(y_sorted.astype(jnp.float32) * wt_sorted[:, None], src_tok, num_segments=n)
    return out.astype(h.dtype)


def setup_inputs(seed: int = 0) -> dict:
    key = jax.random.key(seed)
    ks = jax.random.split(key, 32)
    f32 = jnp.float32
    L, D = DEPTH, D_MODEL

    def nrm(k, shape, scale):
        return jax.random.normal(k, shape, f32) * scale

    def gain(k, shape):
        return 1.0 + 0.02 * jax.random.normal(k, shape, f32)

    offs = jax.random.randint(ks[2], (BATCH, 1), 0, 1024, dtype=jnp.int32)
    positions = (jnp.arange(SEQ, dtype=jnp.int32)[None, :] + offs).astype(jnp.int32)
    dt = jnp.exp(jax.random.uniform(ks[14], (L, DN_HEADS), f32, math.log(1e-3), math.log(1e-1)))
    return {
        'x': nrm(ks[0], (BATCH, SEQ, D), 1.0),
        'c': nrm(ks[1], (BATCH, D), 1.0),
        'positions': positions,
        'w_ada': nrm(ks[3], (L, D, N_MOD * D), 0.5 * D ** -0.5),
        'b_ada': nrm(ks[4], (L, N_MOD * D), 0.02),
        'norm1_g': gain(ks[5], (L, D)),
        'norm2_g': gain(ks[6], (L, D)),
        'w_in': nrm(ks[7], (L, D, IN_COLS), D ** -0.5),
        'gmlp_ln_g': gain(ks[8], (L, GMLP_WIDTH)),
        'gmlp_ln_b': nrm(ks[9], (L, GMLP_WIDTH), 0.02),
        'gmlp_ws': nrm(ks[10], (L, GMLP_GROUPS, GMLP_CHUNK, GMLP_CHUNK), GMLP_CHUNK ** -0.5),
        'gmlp_bs': gain(ks[11], (L, GMLP_GROUPS, GMLP_CHUNK)),
        'dn_conv_w': nrm(ks[12], (L, DN_CONV, 3 * DN_WIDTH), DN_CONV ** -0.5),
        'dn_a_log': jnp.log(jax.random.uniform(ks[13], (L, DN_HEADS), f32, 1.0, 16.0)),
        'dn_dt_bias': dt + jnp.log(-jnp.expm1(-dt)),
        'dn_norm_g': gain(ks[15], (L, DN_HEAD_DIM)),
        'mla_cq_g': gain(ks[16], (L, MLA_Q_RANK)),
        'mla_ckv_g': gain(ks[17], (L, MLA_KV_RANK)),
        'mla_w_uq': nrm(ks[18], (L, MLA_Q_RANK, MLA_HEADS * (MLA_NOPE_DIM + MLA_ROPE_DIM)), MLA_Q_RANK ** -0.5),
        'mla_w_ukv': nrm(ks[19], (L, MLA_KV_RANK, MLA_HEADS * (MLA_NOPE_DIM + MLA_V_DIM)), MLA_KV_RANK ** -0.5),
        'w_br_gmlp': nrm(ks[20], (L, GMLP_WIDTH, D), GMLP_WIDTH ** -0.5),
        'w_br_dn': nrm(ks[21], (L, DN_WIDTH, D), DN_WIDTH ** -0.5),
        'w_br_mla': nrm(ks[22], (L, MLA_WIDTH, D), MLA_WIDTH ** -0.5),
        'w_out': nrm(ks[23], (L, D, D), D ** -0.5),
        'w_router': nrm(ks[24], (D, N_EXPERTS), D ** -0.5),
        'router_bias': nrm(ks[25], (N_EXPERTS,), 0.01),
        'w_gate': nrm(ks[26], (L, N_EXPERTS, D, D_EXPERT), D ** -0.5),
        'w_up': nrm(ks[27], (L, N_EXPERTS, D, D_EXPERT), D ** -0.5),
        'w_down': nrm(ks[28], (L, N_EXPERTS, D_EXPERT, D), D_EXPERT ** -0.5),
        'final_norm_g': gain(ks[29], (D,)),
    }


def reference(x, c, positions, w_ada, b_ada, norm1_g, norm2_g, w_in, gmlp_ln_g, gmlp_ln_b, gmlp_ws,
              gmlp_bs, dn_conv_w, dn_a_log, dn_dt_bias, dn_norm_g, mla_cq_g, mla_ckv_g, mla_w_uq,
              mla_w_ukv, w_br_gmlp, w_br_dn, w_br_mla, w_out, w_router, router_bias, w_gate, w_up,
              w_down, final_norm_g):
    B, T, D = x.shape
    c_act = jax.nn.silu(c)
    for l in range(DEPTH):
        mod = (c_act @ w_ada[l] + b_ada[l]).reshape(B, N_MOD, 1, D)
        shift1, scale1, gate1 = mod[:, 0], mod[:, 1], mod[:, 2]
        shift2, scale2, gate2 = mod[:, 3], mod[:, 4], mod[:, 5]
        h = rmsnorm(x, norm1_g[l]) * (1.0 + scale1) + shift1
        x = x + gate1 * token_mixer(h, positions, w_in[l], gmlp_ln_g[l], gmlp_ln_b[l], gmlp_ws[l], gmlp_bs[l],
                                    dn_conv_w[l], dn_a_log[l], dn_dt_bias[l], dn_norm_g[l], mla_cq_g[l],
                                    mla_ckv_g[l], mla_w_uq[l], mla_w_ukv[l], w_br_gmlp[l], w_br_dn[l],
                                    w_br_mla[l], w_out[l])
        h = (rmsnorm(x, norm2_g[l]) * (1.0 + scale2) + shift2).reshape(B * T, D)
        idx, wts = route(h, w_router, router_bias)
        x = x + gate2 * moe_ffn(h, idx, wts, w_gate[l], w_up[l], w_down[l]).reshape(B, T, D)
    return rmsnorm(x, final_norm_g)
```

```python
import functools
import math

import jax
import jax.numpy as jnp
from jax import lax
from jax.experimental import pallas as pl
from jax.experimental.pallas import tpu as pltpu

F32 = jnp.float32
BF16 = jnp.bfloat16
I32 = jnp.int32

EPS = 1e-6
N_MOD = 6
GM_GROUPS, GM_DIM, GM_CHUNK = 8, 128, 128
GM_WIDTH = GM_GROUPS * GM_DIM
DN_HEADS, DN_DIM, DN_CONV = 8, 128, 4
DN_WIDTH = DN_HEADS * DN_DIM
DN_CHUNK = 128
DN_BASE = 16
ML_HEADS, ML_RANK, ML_NOPE, ML_ROPE, ML_V = 8, 512, 128, 64, 128
ML_QK = ML_NOPE + ML_ROPE
ROPE_THETA = 10000.0
N_EXPERTS, N_GROUPS, PER_GROUP, D_EXPERT = 16, 4, 4, 1024
MOE_ROWS = 256

LANES = 128
OFF_U, OFF_V, OFF_Q, OFF_K, OFF_VV, OFF_Z = 0, 1024, 2048, 3072, 4096, 5120
OFF_GATES = 6144
OFF_CQ = OFF_GATES + 3 * 2048
OFF_CKV = OFF_CQ + ML_RANK
OFF_KR = OFF_CKV + ML_RANK
OFF_BA = OFF_KR + LANES
PROJ_TN = 1536
PROJ_COLS = 13824

VMEM_LIMIT = 52 * 1024 * 1024


def _params(sem, vmem=VMEM_LIMIT):
    return pltpu.CompilerParams(dimension_semantics=sem, vmem_limit_bytes=vmem)


def _ada_kernel(c_ref, w_ref, b_ref, o_ref):
    c = c_ref[...]
    ca = (c * jax.nn.sigmoid(c)).astype(BF16)
    o_ref[0] = jnp.dot(ca, w_ref[0].astype(BF16), preferred_element_type=F32) + b_ref[0]


def _ada_mod(c, w_ada, b_ada):
    L, D, M = w_ada.shape
    B = c.shape[0]
    tn = 1024
    return pl.pallas_call(
        _ada_kernel,
        grid=(L, M // tn),
        in_specs=[pl.BlockSpec((B, D), lambda l, j: (0, 0)),
                  pl.BlockSpec((1, D, tn), lambda l, j: (l, 0, j)),
                  pl.BlockSpec((1, 1, tn), lambda l, j: (l, 0, j))],
        out_specs=pl.BlockSpec((1, B, tn), lambda l, j: (l, 0, j)),
        out_shape=jax.ShapeDtypeStruct((L, B, M), F32),
        compiler_params=_params(("parallel", "parallel")),
        name="ada_mod",
    )(c, w_ada, b_ada.reshape(L, 1, M))


def _inproj_kernel(x_ref, g_ref, sc_ref, sh_ref, w_ref, o_ref, h_ref):
    @pl.when(pl.program_id(1) == 0)
    def _():
        x = x_ref[...]
        y = x * lax.rsqrt(jnp.mean(x * x, axis=-1, keepdims=True) + EPS) * g_ref[...]
        h_ref[...] = (y * (1.0 + sc_ref[0]) + sh_ref[0]).astype(BF16)

    o_ref[...] = jnp.dot(h_ref[...], w_ref[...], preferred_element_type=F32).astype(BF16)


def _inproj(x2, g, mod, w, T):
    N, D = x2.shape
    C = w.shape[1]
    tm = min(1024, T)
    tpb = T // tm
    return pl.pallas_call(
        _inproj_kernel,
        grid=(N // tm, C // PROJ_TN),
        in_specs=[pl.BlockSpec((tm, D), lambda i, j: (i, 0)),
                  pl.BlockSpec((1, D), lambda i, j: (0, 0)),
                  pl.BlockSpec((1, 1, D), lambda i, j: ((i // tpb) * N_MOD + 1, 0, 0)),
                  pl.BlockSpec((1, 1, D), lambda i, j: ((i // tpb) * N_MOD + 0, 0, 0)),
                  pl.BlockSpec((D, PROJ_TN), lambda i, j: (0, j))],
        out_specs=pl.BlockSpec((tm, PROJ_TN), lambda i, j: (i, j)),
        out_shape=jax.ShapeDtypeStruct((N, C), BF16),
        scratch_shapes=[pltpu.VMEM((tm, D), BF16)],
        compiler_params=_params(("parallel", "arbitrary")),
        name="inproj",
    )(x2, g.reshape(1, D), mod, mod, w)


def _gmlp_kernel(u_ref, v_ref, lg_ref, lb_ref, ws_ref, bs_ref, o_ref):
    tm = u_ref.shape[0]
    v = jax.nn.gelu(v_ref[...].astype(F32))
    mu = jnp.mean(v, axis=-1, keepdims=True)
    vc = v - mu
    var = jnp.mean(vc * vc, axis=-1, keepdims=True)
    vn = (vc * lax.rsqrt(var + EPS) * lg_ref[...] + lb_ref[...]).astype(BF16)
    row = lax.broadcasted_iota(I32, (GM_CHUNK, GM_CHUNK), 0)
    col = lax.broadcasted_iota(I32, (GM_CHUNK, GM_CHUNK), 1)
    causal = col <= row
    for g in range(GM_GROUPS):
        w = jnp.where(causal, ws_ref[g], 0.0).astype(BF16)
        lo, hi = g * GM_DIM, (g + 1) * GM_DIM
        for c in range(tm // GM_CHUNK):
            r0, r1 = c * GM_CHUNK, (c + 1) * GM_CHUNK
            s = jnp.dot(w, vn[r0:r1, lo:hi], preferred_element_type=F32) + bs_ref[g]
            u = jax.nn.gelu(u_ref[r0:r1, lo:hi].astype(F32))
            o_ref[r0:r1, lo:hi] = (u * s).astype(BF16)


def _gmlp(proj, ln_g, ln_b, ws, bs, T):
    N = proj.shape[0]
    tm = min(512, T)
    W = GM_WIDTH
    return pl.pallas_call(
        _gmlp_kernel,
        grid=(N // tm,),
        in_specs=[pl.BlockSpec((tm, W), lambda i: (i, OFF_U // W)),
                  pl.BlockSpec((tm, W), lambda i: (i, OFF_V // W)),
                  pl.BlockSpec((1, W), lambda i: (0, 0)),
                  pl.BlockSpec((1, W), lambda i: (0, 0)),
                  pl.BlockSpec((GM_GROUPS, GM_CHUNK, GM_CHUNK), lambda i: (0, 0, 0)),
                  pl.BlockSpec((GM_GROUPS, GM_CHUNK, 1), lambda i: (0, 0, 0))],
        out_specs=pl.BlockSpec((tm, W), lambda i: (i, 0)),
        out_shape=jax.ShapeDtypeStruct((N, W), BF16),
        compiler_params=_params(("parallel",)),
        name="gmlp",
    )(proj, proj, ln_g.reshape(1, W), ln_b.reshape(1, W), ws, bs.reshape(GM_GROUPS, GM_CHUNK, 1))


def _unit_lower_inverse(low):
    C = low.shape[0]
    r = lax.broadcasted_iota(I32, (C, C), 0)
    c = lax.broadcasted_iota(I32, (C, C), 1)
    eye = (r == c).astype(F32)

    def mm(a, b):
        return jnp.dot(a.astype(BF16), b.astype(BF16), preferred_element_type=F32)

    s = DN_BASE
    x = -jnp.where((r // s) == (c // s), low, 0.0)
    t = eye + x
    p = x
    for _ in range(int(math.log2(s)) - 1):
        p = mm(p, p)
        t = t + mm(t, p)
    while s < C:
        off = jnp.where(((r // (2 * s)) == (c // (2 * s))) & ((r // s) != (c // s)), low, 0.0)
        t = t - mm(t, mm(off, t))
        s *= 2
    return t


def _dn_kernel(q_ref, k_ref, v_ref, z_ref, ba_ref, cw_ref, alog_ref, dtb_ref, ng_ref, o_ref,
               ext_ref, st_ref):
    tc = q_ref.shape[0]
    C = DN_CHUNK

    @pl.when(pl.program_id(1) == 0)
    def _():
        ext_ref[:, 0:8, :] = jnp.zeros((3, 8, DN_WIDTH), F32)
        st_ref[...] = jnp.zeros_like(st_ref)

    @pl.when(pl.program_id(1) != 0)
    def _():
        ext_ref[:, 0:8, :] = ext_ref[:, tc:tc + 8, :]

    ext_ref[0, 8:, :] = q_ref[...].astype(F32)
    ext_ref[1, 8:, :] = k_ref[...].astype(F32)
    ext_ref[2, 8:, :] = v_ref[...].astype(F32)

    def conv_silu(i):
        acc = jnp.zeros((tc, DN_WIDTH), F32)
        for kk in range(DN_CONV):
            acc = acc + cw_ref[i, kk:kk + 1, :] * ext_ref[i, pl.ds(8 - (DN_CONV - 1) + kk, tc), :]
        return acc * jax.nn.sigmoid(acc)

    qa, ka, va = conv_silu(0), conv_silu(1), conv_silu(2)

    ba = ba_ref[...].astype(F32)
    beta_all = jax.nn.sigmoid(ba)
    g_all = -jnp.exp(alog_ref[...]) * jax.nn.softplus(ba + dtb_ref[...])
    pos = lax.broadcasted_iota(I32, (tc, LANES), 0) % C
    gc_all = g_all
    sh = 1
    while sh < C:
        gc_all = gc_all + jnp.where(pos >= sh, pltpu.roll(gc_all, sh, 0), 0.0)
        sh *= 2
    gc_rows = gc_all.T

    r = lax.broadcasted_iota(I32, (C, C), 0)
    c = lax.broadcasted_iota(I32, (C, C), 1)
    incl = c <= r
    strict = c < r

    def mm(a, b):
        return jnp.dot(a.astype(BF16), b.astype(BF16), preferred_element_type=F32)

    def mm_nt(a, b):
        return lax.dot_general(a.astype(BF16), b.astype(BF16), (((1,), (1,)), ((), ())),
                               preferred_element_type=F32)

    def mm_tn(a, b):
        return lax.dot_general(a.astype(BF16), b.astype(BF16), (((0,), (0,)), ((), ())),
                               preferred_element_type=F32)

    for h in range(DN_HEADS):
        lo, hi = h * DN_DIM, (h + 1) * DN_DIM
        qh, kh, vh = qa[:, lo:hi], ka[:, lo:hi], va[:, lo:hi]
        qh = qh * lax.rsqrt(jnp.sum(qh * qh, axis=-1, keepdims=True) + EPS) * (DN_DIM ** -0.5)
        kh = kh * lax.rsqrt(jnp.sum(kh * kh, axis=-1, keepdims=True) + EPS)
        beta = beta_all[:, h:h + 1]
        gcol = gc_all[:, DN_HEADS + h:DN_HEADS + h + 1]
        state = st_ref[h]
        for ci in range(tc // C):
            r0, r1 = ci * C, (ci + 1) * C
            q_c, k_c, v_c = qh[r0:r1], kh[r0:r1], vh[r0:r1]
            b_c, g_c = beta[r0:r1], gcol[r0:r1]
            g_r = gc_rows[DN_HEADS + h:DN_HEADS + h + 1, r0:r1]
            g_last = g_r[:, C - 1:C]
            decay = jnp.exp(jnp.where(incl, g_c - g_r, -1e30))
            kb = k_c * b_c
            low = jnp.where(strict, mm_nt(kb, k_c) * decay, 0.0)
            tinv = _unit_lower_inverse(low)
            u_c = mm(tinv, v_c * b_c)
            w_c = mm(tinv, kb * jnp.exp(g_c))
            intra = jnp.where(incl, mm_nt(q_c, k_c) * decay, 0.0)
            v_new = u_c - mm(w_c, state)
            o_c = mm(q_c * jnp.exp(g_c), state) + mm(intra, v_new)
            k_dec = k_c * jnp.exp(g_last - g_c)
            state = state * jnp.exp(g_last) + mm_tn(k_dec, v_new)
            o_n = o_c * lax.rsqrt(jnp.mean(o_c * o_c, axis=-1, keepdims=True) + EPS) * ng_ref[...]
            z = z_ref[r0:r1, lo:hi].astype(F32)
            o_ref[r0:r1, lo:hi] = (o_n * (z * jax.nn.sigmoid(z))).astype(BF16)
        st_ref[h] = state


def _deltanet(proj, conv_w, a_log, dt_bias, norm_g, B, T):
    N = proj.shape[0]
    tc = min(256, T)
    tpb = T // tc
    W = DN_WIDTH
    cw = conv_w.reshape(DN_CONV, 3, W).transpose(1, 0, 2)
    pad = jnp.zeros((LANES - 2 * DN_HEADS,), F32)
    alog = jnp.concatenate([jnp.zeros((DN_HEADS,), F32), a_log, pad]).reshape(1, LANES)
    dtb = jnp.concatenate([jnp.zeros((DN_HEADS,), F32), dt_bias, pad]).reshape(1, LANES)
    blk = lambda off: pl.BlockSpec((tc, W), lambda b, t: (b * tpb + t, off // W))
    return pl.pallas_call(
        _dn_kernel,
        grid=(B, tpb),
        in_specs=[blk(OFF_Q), blk(OFF_K), blk(OFF_VV), blk(OFF_Z),
                  pl.BlockSpec((tc, LANES), lambda b, t: (b * tpb + t, OFF_BA // LANES)),
                  pl.BlockSpec((3, DN_CONV, W), lambda b, t: (0, 0, 0)),
                  pl.BlockSpec((1, LANES), lambda b, t: (0, 0)),
                  pl.BlockSpec((1, LANES), lambda b, t: (0, 0)),
                  pl.BlockSpec((1, DN_DIM), lambda b, t: (0, 0))],
        out_specs=pl.BlockSpec((tc, W), lambda b, t: (b * tpb + t, 0)),
        out_shape=jax.ShapeDtypeStruct((N, W), BF16),
        scratch_shapes=[pltpu.VMEM((3, tc + 8, W), F32),
                        pltpu.VMEM((DN_HEADS, DN_DIM, DN_DIM), F32)],
        compiler_params=_params(("parallel", "arbitrary")),
        name="deltanet",
    )(proj, proj, proj, proj, proj, cw, alog, dtb, norm_g.reshape(1, DN_DIM))


def _mla_proj_kernel(cq_ref, ckv_ref, kr_ref, pos_ref, invf_ref, gq_ref, gkv_ref, wq_ref, wkv_ref,
                     q_ref, k_ref, v_ref, cqn_ref, ckvn_ref, cs_ref):
    @pl.when(pl.program_id(1) == 0)
    def _():
        def rms(x, g):
            return (x * lax.rsqrt(jnp.mean(x * x, axis=-1, keepdims=True) + EPS) * g).astype(BF16)
        cqn_ref[...] = rms(cq_ref[...].astype(F32), gq_ref[...])
        ckvn_ref[...] = rms(ckv_ref[...].astype(F32), gkv_ref[...])
        ang = pos_ref[...].astype(F32) * invf_ref[...]
        cs_ref[0] = jnp.cos(ang)
        cs_ref[1] = jnp.sin(ang)

    cos, sin = cs_ref[0], cs_ref[1]
    scale = ML_QK ** -0.5

    def rope(pair):
        return pair * cos + pltpu.roll(pair, ML_ROPE, 1) * sin

    qa = jnp.dot(cqn_ref[...], wq_ref[0], preferred_element_type=F32)
    q_ref[0, 0, :, 0:ML_NOPE] = (qa[:, 0:ML_NOPE] * scale).astype(BF16)
    q_ref[0, 0, :, ML_NOPE:ML_QK] = (rope(qa[:, ML_NOPE:]) * scale)[:, 0:ML_ROPE].astype(BF16)
    kva = jnp.dot(ckvn_ref[...], wkv_ref[0], preferred_element_type=F32)
    k_ref[0, 0, :, 0:ML_NOPE] = kva[:, 0:ML_NOPE].astype(BF16)
    k_ref[0, 0, :, ML_NOPE:ML_QK] = rope(kr_ref[...].astype(F32))[:, 0:ML_ROPE].astype(BF16)
    v_ref[0, 0] = kva[:, ML_NOPE:].astype(BF16)


def _mla_proj(proj, pos_col, cq_g, ckv_g, w_uq, w_ukv, B, T):
    N = proj.shape[0]
    H, R = ML_HEADS, ML_RANK
    tm = min(512, T)
    tpb = T // tm
    wq = w_uq.reshape(R, H, ML_QK).transpose(1, 0, 2)
    wr = wq[:, :, ML_NOPE:]
    wq = jnp.concatenate([wq, -wr[:, :, ML_ROPE // 2:], wr[:, :, :ML_ROPE // 2]], axis=-1).astype(BF16)
    wkv = w_ukv.reshape(R, H, ML_NOPE + ML_V).transpose(1, 0, 2).astype(BF16)
    half = ROPE_THETA ** (-jnp.arange(0, ML_ROPE, 2, dtype=F32) / ML_ROPE)
    invf = jnp.tile(half, LANES // (ML_ROPE // 2)).reshape(1, LANES)
    hs = lambda d: pl.BlockSpec((1, 1, tm, d), lambda i, h: (i // tpb, h, i % tpb, 0))
    return pl.pallas_call(
        _mla_proj_kernel,
        grid=(N // tm, H),
        in_specs=[pl.BlockSpec((tm, R), lambda i, h: (i, OFF_CQ // R)),
                  pl.BlockSpec((tm, R), lambda i, h: (i, OFF_CKV // R)),
                  pl.BlockSpec((tm, LANES), lambda i, h: (i, OFF_KR // LANES)),
                  pl.BlockSpec((tm, 1), lambda i, h: (i, 0)),
                  pl.BlockSpec((1, LANES), lambda i, h: (0, 0)),
                  pl.BlockSpec((1, R), lambda i, h: (0, 0)),
                  pl.BlockSpec((1, R), lambda i, h: (0, 0)),
                  pl.BlockSpec((1, R, 2 * LANES), lambda i, h: (h, 0, 0)),
                  pl.BlockSpec((1, R, 2 * LANES), lambda i, h: (h, 0, 0))],
        out_specs=[hs(ML_QK), hs(ML_QK), hs(ML_V)],
        out_shape=[jax.ShapeDtypeStruct((B, H, T, ML_QK), BF16),
                   jax.ShapeDtypeStruct((B, H, T, ML_QK), BF16),
                   jax.ShapeDtypeStruct((B, H, T, ML_V), BF16)],
        scratch_shapes=[pltpu.VMEM((tm, R), BF16), pltpu.VMEM((tm, R), BF16),
                        pltpu.VMEM((2, tm, LANES), F32)],
        compiler_params=_params(("parallel", "arbitrary")),
        name="mla_proj",
    )(proj, proj, proj, pos_col, invf, cq_g.reshape(1, R), ckv_g.reshape(1, R), wq, wkv)


def _flash_kernel(q_ref, k_ref, v_ref, o_ref, m_ref, l_ref, acc_ref):
    tq = q_ref.shape[2]
    qi = pl.program_id(2)
    q = q_ref[0, 0]
    m_ref[...] = jnp.full_like(m_ref, -jnp.inf)
    l_ref[...] = jnp.zeros_like(l_ref)
    acc_ref[...] = jnp.zeros_like(acc_ref)

    def step(j, masked):
        start = pl.multiple_of(j * tq, tq)
        k = k_ref[0, 0, pl.ds(start, tq), :]
        v = v_ref[0, 0, pl.ds(start, tq), :]
        s = lax.dot_general(q, k, (((1,), (1,)), ((), ())), preferred_element_type=F32)
        if masked:
            r = lax.broadcasted_iota(I32, (tq, tq), 0)
            c = lax.broadcasted_iota(I32, (tq, tq), 1)
            s = jnp.where(c <= r, s, -jnp.inf)
        m_prev = m_ref[...]
        m_new = jnp.maximum(m_prev, jnp.max(s, axis=-1, keepdims=True))
        alpha = jnp.exp(m_prev - m_new)
        p = jnp.exp(s - m_new)
        l_ref[...] = alpha * l_ref[...] + jnp.sum(p, axis=-1, keepdims=True)
        acc_ref[...] = alpha * acc_ref[...] + jnp.dot(p.astype(BF16), v, preferred_element_type=F32)
        m_ref[...] = m_new

    def body(j, carry):
        step(j, False)
        return carry

    lax.fori_loop(0, qi, body, 0)
    step(qi, True)
    o_ref[0] = (acc_ref[...] / l_ref[...]).astype(BF16)


def _flash(q, k, v):
    B, H, T, _ = q.shape
    tq = min(512, T)
    return pl.pallas_call(
        _flash_kernel,
        grid=(B, H, T // tq),
        in_specs=[pl.BlockSpec((1, 1, tq, ML_QK), lambda b, h, i: (b, h, i, 0)),
                  pl.BlockSpec((1, 1, T, ML_QK), lambda b, h, i: (b, h, 0, 0)),
                  pl.BlockSpec((1, 1, T, ML_V), lambda b, h, i: (b, h, 0, 0))],
        out_specs=pl.BlockSpec((1, tq, ML_V), lambda b, h, i: (b, i, h)),
        out_shape=jax.ShapeDtypeStruct((B, T, H * ML_V), BF16),
        scratch_shapes=[pltpu.VMEM((tq, 1), F32), pltpu.VMEM((tq, 1), F32), pltpu.VMEM((tq, ML_V), F32)],
        compiler_params=_params(("parallel", "parallel", "arbitrary")),
        name="mla_flash",
    )(q, k, v)


def _merge_kernel(ya_ref, yb_ref, yc_ref, ga_ref, gb_ref, gc_ref, wa_ref, wb_ref, wc_ref, o_ref):
    def br(y_ref, g_ref, w_ref):
        return jax.nn.sigmoid(g_ref[...].astype(F32)) * jnp.dot(y_ref[...], w_ref[...],
                                                                 preferred_element_type=F32)
    o_ref[...] = (br(ya_ref, ga_ref, wa_ref) + br(yb_ref, gb_ref, wb_ref)
                  + br(yc_ref, gc_ref, wc_ref)).astype(BF16)


def _merge(ya, yb, yc, proj, wa, wb, wc, T):
    N, W = ya.shape
    D = wa.shape[1]
    tm = min(1024, T)
    tn = 512
    ys = pl.BlockSpec((tm, W), lambda i, j: (i, 0))
    gs = lambda k: pl.BlockSpec((tm, tn), lambda i, j: (i, (OFF_GATES + k * D) // tn + j))
    ws = pl.BlockSpec((W, tn), lambda i, j: (0, j))
    return pl.pallas_call(
        _merge_kernel,
        grid=(N // tm, D // tn),
        in_specs=[ys, ys, ys, gs(0), gs(1), gs(2), ws, ws, ws],
        out_specs=pl.BlockSpec((tm, tn), lambda i, j: (i, j)),
        out_shape=jax.ShapeDtypeStruct((N, D), BF16),
        compiler_params=_params(("parallel", "arbitrary")),
        name="merge",
    )(ya, yb, yc, proj, proj, proj, wa, wb, wc)


def _outproj_kernel(m_ref, w_ref, x_ref, g1_ref, ng_ref, sc_ref, sh_ref, wr_ref, rb_ref,
                    xo_ref, h_ref, e_ref, rk_ref, wt_ref, cnt_ref, carry_ref):
    tm = m_ref.shape[0]

    @pl.when(pl.program_id(0) == 0)
    def _():
        carry_ref[...] = jnp.zeros_like(carry_ref)

    x = x_ref[...] + g1_ref[0] * jnp.dot(m_ref[...], w_ref[...], preferred_element_type=F32)
    xo_ref[...] = x
    y = x * lax.rsqrt(jnp.mean(x * x, axis=-1, keepdims=True) + EPS) * ng_ref[...]
    h = y * (1.0 + sc_ref[0]) + sh_ref[0]
    h_ref[...] = h

    logits = jnp.dot(h, wr_ref[...], preferred_element_type=F32, precision=lax.Precision.HIGHEST)
    lt = logits.T
    sc = [jax.nn.sigmoid(lt[r:r + 1, :]) for r in range(N_EXPERTS)]
    bi = [sc[r] + rb_ref[r:r + 1, :] for r in range(N_EXPERTS)]
    gs = []
    for g in range(N_GROUPS):
        b0, b1, b2, b3 = (bi[j * N_GROUPS + g] for j in range(PER_GROUP))
        hi01, lo01 = jnp.maximum(b0, b1), jnp.minimum(b0, b1)
        hi23, lo23 = jnp.maximum(b2, b3), jnp.minimum(b2, b3)
        gs.append(jnp.maximum(hi01, hi23) + jnp.maximum(jnp.minimum(hi01, hi23), jnp.maximum(lo01, lo23)))
    gmax = jnp.maximum(jnp.maximum(gs[0], gs[1]), jnp.maximum(gs[2], gs[3]))
    best = jnp.where(gs[0] == gmax, 0, jnp.where(gs[1] == gmax, 1, jnp.where(gs[2] == gmax, 2, 3)))

    def pick(vals, idx):
        return jnp.where(idx == 0, vals[0], jnp.where(idx == 1, vals[1], jnp.where(idx == 2, vals[2], vals[3])))

    v = [pick([bi[j * N_GROUPS + g] for g in range(N_GROUPS)], best) for j in range(PER_GROUP)]
    s = [pick([sc[j * N_GROUPS + g] for g in range(N_GROUPS)], best) for j in range(PER_GROUP)]

    def first_max(vals):
        mx = jnp.maximum(jnp.maximum(vals[0], vals[1]), jnp.maximum(vals[2], vals[3]))
        return jnp.where(vals[0] == mx, 0, jnp.where(vals[1] == mx, 1, jnp.where(vals[2] == mx, 2, 3)))

    i1 = first_max(v)
    i2 = first_max([jnp.where(i1 == j, -jnp.inf, v[j]) for j in range(PER_GROUP)])
    s1, s2 = pick(s, i1), pick(s, i2)
    tot = s1 + s2
    e1 = best * PER_GROUP + i1
    e2 = best * PER_GROUP + i2
    e_ref[0:1, :] = e1
    e_ref[1:2, :] = e2

    eio = lax.broadcasted_iota(I32, (N_EXPERTS, tm), 0)
    hit1, hit2 = eio == e1, eio == e2
    cnt = (hit1 | hit2).astype(BF16)
    r = lax.broadcasted_iota(I32, (tm, tm), 0)
    c = lax.broadcasted_iota(I32, (tm, tm), 1)
    before = (r < c).astype(BF16)
    run = jnp.dot(cnt, before, preferred_element_type=F32) + carry_ref[:, 0:1]
    rk_ref[0:1, :] = jnp.sum(jnp.where(hit1, run, 0.0), axis=0, keepdims=True).astype(I32)
    rk_ref[1:2, :] = jnp.sum(jnp.where(hit2, run, 0.0), axis=0, keepdims=True).astype(I32)
    carry_ref[...] = carry_ref[...] + jnp.sum(cnt.astype(F32), axis=1, keepdims=True)
    cnt_ref[...] = carry_ref[...]

    rid = lax.broadcasted_iota(I32, (LANES, tm), 0)
    wrow = jnp.where(rid == 0, s1 / tot, jnp.where(rid == 1, s2 / tot, 0.0))
    wt_ref[...] = wrow.T[:, 0:8]


def _outproj(merged, w_out, x2, mod, norm_g, wr, rb, T):
    N, D = x2.shape
    tm = min(512, T)
    tpb = T // tm
    mspec = lambda k: pl.BlockSpec((1, 1, D), lambda i: ((i // tpb) * N_MOD + k, 0, 0))
    return pl.pallas_call(
        _outproj_kernel,
        grid=(N // tm,),
        in_specs=[pl.BlockSpec((tm, D), lambda i: (i, 0)),
                  pl.BlockSpec((D, D), lambda i: (0, 0), pipeline_mode=pl.Buffered(1)),
                  pl.BlockSpec((tm, D), lambda i: (i, 0)),
                  mspec(2),
                  pl.BlockSpec((1, D), lambda i: (0, 0)),
                  mspec(4), mspec(3),
                  pl.BlockSpec((D, LANES), lambda i: (0, 0)),
                  pl.BlockSpec((N_EXPERTS, 1), lambda i: (0, 0))],
        out_specs=[pl.BlockSpec((tm, D), lambda i: (i, 0)),
                   pl.BlockSpec((tm, D), lambda i: (i, 0)),
                   pl.BlockSpec((2, tm), lambda i: (0, i)),
                   pl.BlockSpec((2, tm), lambda i: (0, i)),
                   pl.BlockSpec((tm, 8), lambda i: (i, 0)),
                   pl.BlockSpec((N_EXPERTS, LANES), lambda i: (0, 0))],
        out_shape=[jax.ShapeDtypeStruct((N, D), F32),
                   jax.ShapeDtypeStruct((N, D), F32),
                   jax.ShapeDtypeStruct((2, N), I32),
                   jax.ShapeDtypeStruct((2, N), I32),
                   jax.ShapeDtypeStruct((N, 8), F32),
                   jax.ShapeDtypeStruct((N_EXPERTS, LANES), F32)],
        scratch_shapes=[pltpu.VMEM((N_EXPERTS, LANES), F32)],
        compiler_params=_params(("arbitrary",)),
        name="outproj_router",
    )(merged, w_out, x2, mod, norm_g.reshape(1, D), mod, mod, wr, rb)


def _dispatch_kernel(dest_ref, h_ref, xs_in_ref, xs_ref, sem):
    del xs_in_ref
    tm = h_ref.shape[0]

    def row_copy(t, k):
        return pltpu.make_async_copy(h_ref.at[pl.ds(t, 1), :], xs_ref.at[pl.ds(dest_ref[k, t], 1), :], sem)

    def issue(t, carry):
        row_copy(t, 0).start()
        row_copy(t, 1).start()
        return carry

    def drain(t, carry):
        row_copy(t, 0).wait()
        row_copy(t, 1).wait()
        return carry

    lax.fori_loop(0, tm, issue, 0)
    lax.fori_loop(0, tm, drain, 0)


def _dispatch(dest, h, n_rows, T):
    N, D = h.shape
    tm = min(512, T)
    xs0 = jnp.zeros((n_rows, D), F32)
    return pl.pallas_call(
        _dispatch_kernel,
        grid=(N // tm,),
        in_specs=[pl.BlockSpec((2, tm), lambda i: (0, i), memory_space=pltpu.SMEM),
                  pl.BlockSpec((tm, D), lambda i: (i, 0)),
                  pl.BlockSpec(memory_space=pl.ANY)],
        out_specs=pl.BlockSpec(memory_space=pl.ANY),
        out_shape=jax.ShapeDtypeStruct((n_rows, D), F32),
        scratch_shapes=[pltpu.SemaphoreType.DMA(())],
        input_output_aliases={2: 0},
        compiler_params=_params(("arbitrary",)),
        name="moe_dispatch",
    )(dest, h, xs0)


def _experts_kernel(be_ref, nb_ref, x_ref, wg_ref, wu_ref, wd_ref, y_ref):
    @pl.when(pl.program_id(0) < nb_ref[0])
    def _():
        x = x_ref[...].astype(BF16)
        a = jnp.dot(x, wg_ref[0], preferred_element_type=F32)
        b = jnp.dot(x, wu_ref[0], preferred_element_type=F32)
        hmid = (a * jax.nn.sigmoid(a) * b).astype(BF16)
        y_ref[...] = jnp.dot(hmid, wd_ref[0], preferred_element_type=F32)

    @pl.when(pl.program_id(0) >= nb_ref[0])
    def _():
        y_ref[...] = jnp.zeros_like(y_ref)


def _experts(blk_e, n_used, xs, wg, wu, wd):
    R, D = xs.shape
    F = wg.shape[2]
    nb = R // MOE_ROWS
    row = lambda i, be, nu: (jnp.minimum(i, nu[0] - 1), 0)
    return pl.pallas_call(
        _experts_kernel,
        grid_spec=pltpu.PrefetchScalarGridSpec(
            num_scalar_prefetch=2,
            grid=(nb,),
            in_specs=[pl.BlockSpec((MOE_ROWS, D), row),
                      pl.BlockSpec((1, D, F), lambda i, be, nu: (be[i], 0, 0)),
                      pl.BlockSpec((1, D, F), lambda i, be, nu: (be[i], 0, 0)),
                      pl.BlockSpec((1, F, D), lambda i, be, nu: (be[i], 0, 0))],
            out_specs=pl.BlockSpec((MOE_ROWS, D), lambda i, be, nu: (i, 0))),
        out_shape=jax.ShapeDtypeStruct((R, D), F32),
        compiler_params=_params(("arbitrary",)),
        name="moe_experts",
    )(blk_e, n_used, xs, wg, wu, wd)


def _combine_kernel(dest_ref, y_ref, wt_ref, x_ref, g2_ref, fg_ref, o_ref, buf_ref, sem, *, final):
    tm = x_ref.shape[0]

    def row_copy(t, k):
        return pltpu.make_async_copy(y_ref.at[pl.ds(dest_ref[k, t], 1), :], buf_ref.at[k, pl.ds(t, 1), :], sem)

    def issue(t, carry):
        row_copy(t, 0).start()
        row_copy(t, 1).start()
        return carry

    def drain(t, carry):
        row_copy(t, 0).wait()
        row_copy(t, 1).wait()
        return carry

    lax.fori_loop(0, tm, issue, 0)
    lax.fori_loop(0, tm, drain, 0)
    wt = wt_ref[...]
    moe = buf_ref[0] * wt[:, 0:1] + buf_ref[1] * wt[:, 1:2]
    x = x_ref[...] + g2_ref[0] * moe
    if final:
        x = x * lax.rsqrt(jnp.mean(x * x, axis=-1, keepdims=True) + EPS) * fg_ref[...]
    o_ref[...] = x


def _combine(dest, y, wt, x2, mod, final_g, T, final):
    N, D = x2.shape
    tm = min(256, T)
    tpb = T // tm
    return pl.pallas_call(
        functools.partial(_combine_kernel, final=final),
        grid=(N // tm,),
        in_specs=[pl.BlockSpec((2, tm), lambda i: (0, i), memory_space=pltpu.SMEM),
                  pl.BlockSpec(memory_space=pl.ANY),
                  pl.BlockSpec((tm, 8), lambda i: (i, 0)),
                  pl.BlockSpec((tm, D), lambda i: (i, 0)),
                  pl.BlockSpec((1, 1, D), lambda i: ((i // tpb) * N_MOD + 5, 0, 0)),
                  pl.BlockSpec((1, D), lambda i: (0, 0))],
        out_specs=pl.BlockSpec((tm, D), lambda i: (i, 0)),
        out_shape=jax.ShapeDtypeStruct((N, D), F32),
        scratch_shapes=[pltpu.VMEM((2, tm, D), F32), pltpu.SemaphoreType.DMA(())],
        compiler_params=_params(("arbitrary",)),
        name="moe_combine",
    )(dest, y, wt, x2, mod, final_g.reshape(1, D))


def _pack_w_in(w):
    D = w.shape[0]
    w = w.astype(BF16)
    widths = (GM_WIDTH, GM_WIDTH, 3 * DN_WIDTH, DN_WIDTH, DN_HEADS, DN_HEADS, ML_RANK, ML_RANK, ML_ROPE,
              3 * D)
    offs = [0]
    for wd in widths:
        offs.append(offs[-1] + wd)
    u, v, qkv, z, b, a, cq, ckv, kr, gates = (w[:, offs[i]:offs[i + 1]] for i in range(len(widths)))
    kr_rot = jnp.concatenate([-kr[:, ML_ROPE // 2:], kr[:, :ML_ROPE // 2]], axis=1)
    zeros = lambda n: jnp.zeros((D, n), BF16)
    packed = jnp.concatenate([u, v, qkv, z, gates, cq, ckv, kr, kr_rot, b, a,
                              zeros(LANES - 2 * DN_HEADS), zeros(PROJ_COLS - OFF_BA - LANES)], axis=1)
    return packed


def _block_table(counts, n_blocks):
    per = (counts + MOE_ROWS - 1) // MOE_ROWS
    ends = jnp.cumsum(per)
    first_row = (ends - per) * MOE_ROWS
    blk_e = jnp.minimum(jnp.searchsorted(ends, jnp.arange(n_blocks, dtype=I32), side="right"),
                        N_EXPERTS - 1).astype(I32)
    return blk_e, ends[-1:].astype(I32), first_row.astype(I32)


def kernel(x, c, positions, w_ada, b_ada, norm1_g, norm2_g, w_in, gmlp_ln_g, gmlp_ln_b, gmlp_ws, gmlp_bs, dn_conv_w, dn_a_log, dn_dt_bias, dn_norm_g, mla_cq_g, mla_ckv_g, mla_w_uq, mla_w_ukv, w_br_gmlp, w_br_dn, w_br_mla, w_out, w_router, router_bias, w_gate, w_up, w_down, final_norm_g):
    B, T, D = x.shape
    L = w_in.shape[0]
    N = B * T
    x2 = x.reshape(N, D)
    pos_col = positions.reshape(N, 1)
    mod_all = _ada_mod(c, w_ada, b_ada).reshape(L, B * N_MOD, 1, D)

    perm = jnp.array([(r % N_GROUPS) * PER_GROUP + r // N_GROUPS for r in range(N_EXPERTS)], I32)
    wr = jnp.concatenate([w_router[:, perm], jnp.zeros((D, LANES - N_EXPERTS), F32)], axis=1)
    rb = router_bias[perm].reshape(N_EXPERTS, 1)
    n_blocks = (2 * N) // MOE_ROWS + N_EXPERTS

    for l in range(L):
        mod = mod_all[l]
        proj = _inproj(x2, norm1_g[l], mod, _pack_w_in(w_in[l]), T)
        ya = _gmlp(proj, gmlp_ln_g[l], gmlp_ln_b[l], gmlp_ws[l], gmlp_bs[l], T)
        yb = _deltanet(proj, dn_conv_w[l], dn_a_log[l], dn_dt_bias[l], dn_norm_g[l], B, T)
        q, k, v = _mla_proj(proj, pos_col, mla_cq_g[l], mla_ckv_g[l], mla_w_uq[l], mla_w_ukv[l], B, T)
        yc = _flash(q, k, v).reshape(N, ML_HEADS * ML_V)
        merged = _merge(ya, yb, yc, proj, w_br_gmlp[l].astype(BF16), w_br_dn[l].astype(BF16),
                        w_br_mla[l].astype(BF16), T)
        x2, h2, e, rank, wt, counts = _outproj(merged, w_out[l].astype(BF16), x2, mod, norm2_g[l], wr, rb, T)
        blk_e, n_used, first_row = _block_table(counts[:, 0].astype(I32), n_blocks)
        dest = first_row[e] + rank
        xs = _dispatch(dest, h2, n_blocks * MOE_ROWS, T)
        y = _experts(blk_e, n_used, xs, w_gate[l].astype(BF16), w_up[l].astype(BF16), w_down[l].astype(BF16))
        x2 = _combine(dest, y, wt, x2, mod, final_norm_g, T, final=(l == L - 1))
    return x2.reshape(B, T, D)
```

```python
import functools
import math

import jax
import jax.numpy as jnp
from jax import lax
from jax.experimental import pallas as pl
from jax.experimental.pallas import tpu as pltpu

F32 = jnp.float32
BF16 = jnp.bfloat16
I32 = jnp.int32

EPS = 1e-6
N_MOD = 6
GM_GROUPS, GM_DIM, GM_CHUNK = 8, 128, 128
GM_WIDTH = GM_GROUPS * GM_DIM
DN_HEADS, DN_DIM, DN_CONV = 8, 128, 4
DN_WIDTH = DN_HEADS * DN_DIM
DN_CHUNK = 128
DN_BASE = 16
ML_HEADS, ML_RANK, ML_NOPE, ML_ROPE, ML_V = 8, 512, 128, 64, 128
ML_QK = ML_NOPE + ML_ROPE
ROPE_THETA = 10000.0
N_EXPERTS, N_GROUPS, PER_GROUP, D_EXPERT = 16, 4, 4, 1024
MOE_ROWS = 256

LANES = 128
OFF_U, OFF_V, OFF_Q, OFF_K, OFF_VV, OFF_Z = 0, 1024, 2048, 3072, 4096, 5120
OFF_GATES = 6144
OFF_CQ = OFF_GATES + 3 * 2048
OFF_CKV = OFF_CQ + ML_RANK
OFF_KR = OFF_CKV + ML_RANK
OFF_BA = OFF_KR + LANES
PROJ_TN = 1536
PROJ_COLS = 13824

VMEM_LIMIT = 52 * 1024 * 1024


def _params(sem, vmem=VMEM_LIMIT):
    return pltpu.CompilerParams(dimension_semantics=sem, vmem_limit_bytes=vmem)


def _ada_kernel(c_ref, w_ref, b_ref, o_ref):
    c = c_ref[...]
    ca = (c * jax.nn.sigmoid(c)).astype(BF16)
    o_ref[0] = jnp.dot(ca, w_ref[0].astype(BF16), preferred_element_type=F32) + b_ref[0]


def _ada_mod(c, w_ada, b_ada):
    L, D, M = w_ada.shape
    B = c.shape[0]
    tn = 1024
    return pl.pallas_call(
        _ada_kernel,
        grid=(L, M // tn),
        in_specs=[pl.BlockSpec((B, D), lambda l, j: (0, 0)),
                  pl.BlockSpec((1, D, tn), lambda l, j: (l, 0, j)),
                  pl.BlockSpec((1, 1, tn), lambda l, j: (l, 0, j))],
        out_specs=pl.BlockSpec((1, B, tn), lambda l, j: (l, 0, j)),
        out_shape=jax.ShapeDtypeStruct((L, B, M), F32),
        compiler_params=_params(("parallel", "parallel")),
        name="ada_mod",
    )(c, w_ada, b_ada.reshape(L, 1, M))


def _inproj_kernel(x_ref, g_ref, sc_ref, sh_ref, w_ref, o_ref, h_ref):
    @pl.when(pl.program_id(1) == 0)
    def _():
        x = x_ref[...]
        y = x * lax.rsqrt(jnp.mean(x * x, axis=-1, keepdims=True) + EPS) * g_ref[...]
        h_ref[...] = (y * (1.0 + sc_ref[0]) + sh_ref[0]).astype(BF16)

    o_ref[...] = jnp.dot(h_ref[...], w_ref[...], preferred_element_type=F32).astype(BF16)


def _inproj(x2, g, mod, w, T):
    N, D = x2.shape
    C = w.shape[1]
    tm = min(1024, T)
    tpb = T // tm
    return pl.pallas_call(
        _inproj_kernel,
        grid=(N // tm, C // PROJ_TN),
        in_specs=[pl.BlockSpec((tm, D), lambda i, j: (i, 0)),
                  pl.BlockSpec((1, D), lambda i, j: (0, 0)),
                  pl.BlockSpec((1, 1, D), lambda i, j: ((i // tpb) * N_MOD + 1, 0, 0)),
                  pl.BlockSpec((1, 1, D), lambda i, j: ((i // tpb) * N_MOD + 0, 0, 0)),
                  pl.BlockSpec((D, PROJ_TN), lambda i, j: (0, j))],
        out_specs=pl.BlockSpec((tm, PROJ_TN), lambda i, j: (i, j)),
        out_shape=jax.ShapeDtypeStruct((N, C), BF16),
        scratch_shapes=[pltpu.VMEM((tm, D), BF16)],
        compiler_params=_params(("parallel", "arbitrary")),
        name="inproj",
    )(x2, g.reshape(1, D), mod, mod, w)


def _gmlp_kernel(u_ref, v_ref, lg_ref, lb_ref, ws_ref, bs_ref, o_ref):
    tm = u_ref.shape[0]
    v = jax.nn.gelu(v_ref[...].astype(F32))
    mu = jnp.mean(v, axis=-1, keepdims=True)
    vc = v - mu
    var = jnp.mean(vc * vc, axis=-1, keepdims=True)
    vn = (vc * lax.rsqrt(var + EPS) * lg_ref[...] + lb_ref[...]).astype(BF16)
    row = lax.broadcasted_iota(I32, (GM_CHUNK, GM_CHUNK), 0)
    col = lax.broadcasted_iota(I32, (GM_CHUNK, GM_CHUNK), 1)
    causal = col <= row
    for g in range(GM_GROUPS):
        w = jnp.where(causal, ws_ref[g], 0.0).astype(BF16)
        lo, hi = g * GM_DIM, (g + 1) * GM_DIM
        for c in range(tm // GM_CHUNK):
            r0, r1 = c * GM_CHUNK, (c + 1) * GM_CHUNK
            s = jnp.dot(w, vn[r0:r1, lo:hi], preferred_element_type=F32) + bs_ref[g]
            u = jax.nn.gelu(u_ref[r0:r1, lo:hi].astype(F32))
            o_ref[r0:r1, lo:hi] = (u * s).astype(BF16)


def _gmlp(proj, ln_g, ln_b, ws, bs, T):
    N = proj.shape[0]
    tm = min(512, T)
    W = GM_WIDTH
    return pl.pallas_call(
        _gmlp_kernel,
        grid=(N // tm,),
        in_specs=[pl.BlockSpec((tm, W), lambda i: (i, OFF_U // W)),
                  pl.BlockSpec((tm, W), lambda i: (i, OFF_V // W)),
                  pl.BlockSpec((1, W), lambda i: (0, 0)),
                  pl.BlockSpec((1, W), lambda i: (0, 0)),
                  pl.BlockSpec((GM_GROUPS, GM_CHUNK, GM_CHUNK), lambda i: (0, 0, 0)),
                  pl.BlockSpec((GM_GROUPS, GM_CHUNK, 1), lambda i: (0, 0, 0))],
        out_specs=pl.BlockSpec((tm, W), lambda i: (i, 0)),
        out_shape=jax.ShapeDtypeStruct((N, W), BF16),
        compiler_params=_params(("parallel",)),
        name="gmlp",
    )(proj, proj, ln_g.reshape(1, W), ln_b.reshape(1, W), ws, bs.reshape(GM_GROUPS, GM_CHUNK, 1))


def _mm(a, b):
    return jnp.dot(a.astype(BF16), b.astype(BF16), preferred_element_type=F32)


def _mm_nt(a, b):
    return lax.dot_general(a.astype(BF16), b.astype(BF16), (((1,), (1,)), ((), ())),
                           preferred_element_type=F32)


def _mm_tn(a, b):
    return lax.dot_general(a.astype(BF16), b.astype(BF16), (((0,), (0,)), ((), ())),
                           preferred_element_type=F32)


def _unit_lower_inverses(lows):
    C = lows[0].shape[0]
    n = range(len(lows))
    r = lax.broadcasted_iota(I32, (C, C), 0)
    c = lax.broadcasted_iota(I32, (C, C), 1)
    eye = (r == c).astype(F32)
    s = DN_BASE
    diag = (r // s) == (c // s)
    p = [-jnp.where(diag, lows[i], 0.0) for i in n]
    t = [eye + p[i] for i in n]
    for _ in range(int(math.log2(s)) - 1):
        p = [_mm(p[i], p[i]) for i in n]
        tp = [_mm(t[i], p[i]) for i in n]
        t = [t[i] + tp[i] for i in n]
    while s < C:
        band = ((r // (2 * s)) == (c // (2 * s))) & ((r // s) != (c // s))
        ot = [_mm(jnp.where(band, lows[i], 0.0), t[i]) for i in n]
        tot = [_mm(t[i], ot[i]) for i in n]
        t = [t[i] - tot[i] for i in n]
        s *= 2
    return t


def _dn_kernel(q_ref, k_ref, v_ref, z_ref, ba_ref, cw_ref, alog_ref, dtb_ref, ng_ref, o_ref,
               ext_ref, st_ref):
    tc = q_ref.shape[0]
    C = DN_CHUNK

    @pl.when(pl.program_id(1) == 0)
    def _():
        ext_ref[:, 0:8, :] = jnp.zeros((3, 8, DN_WIDTH), F32)
        st_ref[...] = jnp.zeros_like(st_ref)

    @pl.when(pl.program_id(1) != 0)
    def _():
        ext_ref[:, 0:8, :] = ext_ref[:, tc:tc + 8, :]

    ext_ref[0, 8:, :] = q_ref[...].astype(F32)
    ext_ref[1, 8:, :] = k_ref[...].astype(F32)
    ext_ref[2, 8:, :] = v_ref[...].astype(F32)

    def conv_silu(i):
        acc = jnp.zeros((tc, DN_WIDTH), F32)
        for kk in range(DN_CONV):
            acc = acc + cw_ref[i, kk:kk + 1, :] * ext_ref[i, pl.ds(8 - (DN_CONV - 1) + kk, tc), :]
        return acc * jax.nn.sigmoid(acc)

    qa, ka, va = conv_silu(0), conv_silu(1), conv_silu(2)

    ba = ba_ref[...].astype(F32)
    beta_all = jax.nn.sigmoid(ba)
    g_all = -jnp.exp(alog_ref[...]) * jax.nn.softplus(ba + dtb_ref[...])
    pos = lax.broadcasted_iota(I32, (tc, LANES), 0) % C
    gc_all = g_all
    sh = 1
    while sh < C:
        gc_all = gc_all + jnp.where(pos >= sh, pltpu.roll(gc_all, sh, 0), 0.0)
        sh *= 2
    gc_rows = gc_all.T

    r = lax.broadcasted_iota(I32, (C, C), 0)
    c = lax.broadcasted_iota(I32, (C, C), 1)
    incl = c <= r
    strict = c < r

    H = DN_HEADS
    nc = tc // C
    idx = [(ci, h) for ci in range(nc) for h in range(H)]
    G = range(len(idx))

    def piece(x, ci, h):
        return x[ci * C:(ci + 1) * C, h * DN_DIM:(h + 1) * DN_DIM]

    def l2n(x):
        return x * lax.rsqrt(jnp.sum(x * x, axis=-1, keepdims=True) + EPS)

    q = [l2n(piece(qa, ci, h)) * (DN_DIM ** -0.5) for ci, h in idx]
    k = [l2n(piece(ka, ci, h)) for ci, h in idx]
    v = [piece(va, ci, h) for ci, h in idx]
    beta = [beta_all[ci * C:(ci + 1) * C, h:h + 1] for ci, h in idx]
    g_c = [gc_all[ci * C:(ci + 1) * C, H + h:H + h + 1] for ci, h in idx]
    g_r = [gc_rows[H + h:H + h + 1, ci * C:(ci + 1) * C] for ci, h in idx]
    g_last = [g_r[i][:, C - 1:C] for i in G]
    decay = [jnp.exp(jnp.where(incl, g_c[i] - g_r[i], -1e30)) for i in G]
    kb = [k[i] * beta[i] for i in G]
    kk = [_mm_nt(kb[i], k[i]) for i in G]
    qk = [_mm_nt(q[i], k[i]) for i in G]
    tinv = _unit_lower_inverses([jnp.where(strict, kk[i] * decay[i], 0.0) for i in G])
    intra = [jnp.where(incl, qk[i] * decay[i], 0.0) for i in G]
    eg = [jnp.exp(g_c[i]) for i in G]
    u = [_mm(tinv[i], v[i] * beta[i]) for i in G]
    w = [_mm(tinv[i], kb[i] * eg[i]) for i in G]
    qg = [q[i] * eg[i] for i in G]
    k_dec = [k[i] * jnp.exp(g_last[i] - g_c[i]) for i in G]

    state = [st_ref[h] for h in range(H)]
    for ci in range(nc):
        ids = [ci * H + h for h in range(H)]
        ws = [_mm(w[i], state[h]) for h, i in enumerate(ids)]
        qs = [_mm(qg[i], state[h]) for h, i in enumerate(ids)]
        v_new = [u[i] - ws[h] for h, i in enumerate(ids)]
        av = [_mm(intra[i], v_new[h]) for h, i in enumerate(ids)]
        kv = [_mm_tn(k_dec[i], v_new[h]) for h, i in enumerate(ids)]
        state = [state[h] * jnp.exp(g_last[i]) + kv[h] for h, i in enumerate(ids)]
        for h in range(H):
            o_c = qs[h] + av[h]
            o_n = o_c * lax.rsqrt(jnp.mean(o_c * o_c, axis=-1, keepdims=True) + EPS) * ng_ref[...]
            z = piece(z_ref, ci, h).astype(F32)
            o_ref[ci * C:(ci + 1) * C, h * DN_DIM:(h + 1) * DN_DIM] = (o_n * (z * jax.nn.sigmoid(z))).astype(BF16)
    for h in range(H):
        st_ref[h] = state[h]


def _deltanet(proj, conv_w, a_log, dt_bias, norm_g, B, T):
    N = proj.shape[0]
    tc = min(256, T)
    tpb = T // tc
    W = DN_WIDTH
    cw = conv_w.reshape(DN_CONV, 3, W).transpose(1, 0, 2)
    pad = jnp.zeros((LANES - 2 * DN_HEADS,), F32)
    alog = jnp.concatenate([jnp.zeros((DN_HEADS,), F32), a_log, pad]).reshape(1, LANES)
    dtb = jnp.concatenate([jnp.zeros((DN_HEADS,), F32), dt_bias, pad]).reshape(1, LANES)
    blk = lambda off: pl.BlockSpec((tc, W), lambda b, t: (b * tpb + t, off // W))
    return pl.pallas_call(
        _dn_kernel,
        grid=(B, tpb),
        in_specs=[blk(OFF_Q), blk(OFF_K), blk(OFF_VV), blk(OFF_Z),
                  pl.BlockSpec((tc, LANES), lambda b, t: (b * tpb + t, OFF_BA // LANES)),
                  pl.BlockSpec((3, DN_CONV, W), lambda b, t: (0, 0, 0)),
                  pl.BlockSpec((1, LANES), lambda b, t: (0, 0)),
                  pl.BlockSpec((1, LANES), lambda b, t: (0, 0)),
                  pl.BlockSpec((1, DN_DIM), lambda b, t: (0, 0))],
        out_specs=pl.BlockSpec((tc, W), lambda b, t: (b * tpb + t, 0)),
        out_shape=jax.ShapeDtypeStruct((N, W), BF16),
        scratch_shapes=[pltpu.VMEM((3, tc + 8, W), F32),
                        pltpu.VMEM((DN_HEADS, DN_DIM, DN_DIM), F32)],
        compiler_params=_params(("parallel", "arbitrary")),
        name="deltanet",
    )(proj, proj, proj, proj, proj, cw, alog, dtb, norm_g.reshape(1, DN_DIM))


def _mla_proj_kernel(cq_ref, ckv_ref, kr_ref, pos_ref, invf_ref, gq_ref, gkv_ref, wq_ref, wkv_ref,
                     q_ref, k_ref, v_ref, cqn_ref, ckvn_ref, cs_ref):
    @pl.when(pl.program_id(1) == 0)
    def _():
        def rms(x, g):
            return (x * lax.rsqrt(jnp.mean(x * x, axis=-1, keepdims=True) + EPS) * g).astype(BF16)
        cqn_ref[...] = rms(cq_ref[...].astype(F32), gq_ref[...])
        ckvn_ref[...] = rms(ckv_ref[...].astype(F32), gkv_ref[...])
        ang = pos_ref[...].astype(F32) * invf_ref[...]
        cs_ref[0] = jnp.cos(ang)
        cs_ref[1] = jnp.sin(ang)

    cos, sin = cs_ref[0], cs_ref[1]
    scale = ML_QK ** -0.5 * math.log2(math.e)

    def rope(pair):
        return pair * cos + pltpu.roll(pair, ML_ROPE, 1) * sin

    qa = jnp.dot(cqn_ref[...], wq_ref[0], preferred_element_type=F32)
    q_ref[0, 0, :, 0:ML_NOPE] = (qa[:, 0:ML_NOPE] * scale).astype(BF16)
    q_ref[0, 0, :, ML_NOPE:ML_QK] = (rope(qa[:, ML_NOPE:]) * scale)[:, 0:ML_ROPE].astype(BF16)
    kva = jnp.dot(ckvn_ref[...], wkv_ref[0], preferred_element_type=F32)
    k_ref[0, 0, :, 0:ML_NOPE] = kva[:, 0:ML_NOPE].astype(BF16)
    k_ref[0, 0, :, ML_NOPE:ML_QK] = rope(kr_ref[...].astype(F32))[:, 0:ML_ROPE].astype(BF16)
    v_ref[0, 0, :, 0:ML_V] = kva[:, ML_NOPE:].astype(BF16)
    v_ref[0, 0, :, ML_V:] = jnp.ones((kva.shape[0], ML_V), BF16)


def _mla_proj(proj, pos_col, cq_g, ckv_g, w_uq, w_ukv, B, T):
    N = proj.shape[0]
    H, R = ML_HEADS, ML_RANK
    tm = min(512, T)
    tpb = T // tm
    wq = w_uq.reshape(R, H, ML_QK).transpose(1, 0, 2)
    wr = wq[:, :, ML_NOPE:]
    wq = jnp.concatenate([wq, -wr[:, :, ML_ROPE // 2:], wr[:, :, :ML_ROPE // 2]], axis=-1).astype(BF16)
    wkv = w_ukv.reshape(R, H, ML_NOPE + ML_V).transpose(1, 0, 2).astype(BF16)
    half = ROPE_THETA ** (-jnp.arange(0, ML_ROPE, 2, dtype=F32) / ML_ROPE)
    invf = jnp.tile(half, LANES // (ML_ROPE // 2)).reshape(1, LANES)
    hs = lambda d: pl.BlockSpec((1, 1, tm, d), lambda i, h: (i // tpb, h, i % tpb, 0))
    return pl.pallas_call(
        _mla_proj_kernel,
        grid=(N // tm, H),
        in_specs=[pl.BlockSpec((tm, R), lambda i, h: (i, OFF_CQ // R)),
                  pl.BlockSpec((tm, R), lambda i, h: (i, OFF_CKV // R)),
                  pl.BlockSpec((tm, LANES), lambda i, h: (i, OFF_KR // LANES)),
                  pl.BlockSpec((tm, 1), lambda i, h: (i, 0)),
                  pl.BlockSpec((1, LANES), lambda i, h: (0, 0)),
                  pl.BlockSpec((1, R), lambda i, h: (0, 0)),
                  pl.BlockSpec((1, R), lambda i, h: (0, 0)),
                  pl.BlockSpec((1, R, 2 * LANES), lambda i, h: (h, 0, 0)),
                  pl.BlockSpec((1, R, 2 * LANES), lambda i, h: (h, 0, 0))],
        out_specs=[hs(ML_QK), hs(ML_QK), hs(2 * ML_V)],
        out_shape=[jax.ShapeDtypeStruct((B, H, T, ML_QK), BF16),
                   jax.ShapeDtypeStruct((B, H, T, ML_QK), BF16),
                   jax.ShapeDtypeStruct((B, H, T, 2 * ML_V), BF16)],
        scratch_shapes=[pltpu.VMEM((tm, R), BF16), pltpu.VMEM((tm, R), BF16),
                        pltpu.VMEM((2, tm, LANES), F32)],
        compiler_params=_params(("parallel", "arbitrary")),
        name="mla_proj",
    )(proj, proj, proj, pos_col, invf, cq_g.reshape(1, R), ckv_g.reshape(1, R), wq, wkv)


def _flash_kernel(q_ref, k_ref, v_ref, o_ref, m_ref, l_ref, acc_ref):
    hp, tq = q_ref.shape[1], q_ref.shape[2]
    heads = range(hp)
    qi = pl.program_id(2)
    m_ref[...] = jnp.full_like(m_ref, -jnp.inf)
    l_ref[...] = jnp.zeros_like(l_ref)
    acc_ref[...] = jnp.zeros_like(acc_ref)

    def step(j, masked):
        start = pl.multiple_of(j * tq, tq)
        s = [lax.dot_general(q_ref[0, h], k_ref[0, h, pl.ds(start, tq), :], (((1,), (1,)), ((), ())),
                             preferred_element_type=F32) for h in heads]
        if masked:
            r = lax.broadcasted_iota(I32, (tq, tq), 0)
            c = lax.broadcasted_iota(I32, (tq, tq), 1)
            s = [jnp.where(c <= r, s[h], -jnp.inf) for h in heads]
        m_prev = [m_ref[h] for h in heads]
        m_new = [jnp.maximum(m_prev[h], jnp.max(s[h], axis=-1, keepdims=True)) for h in heads]
        p = [jnp.exp2(s[h] - jnp.concatenate([m_new[h]] * (tq // LANES), axis=1)) for h in heads]
        alpha = [jnp.exp2(m_prev[h] - m_new[h]) for h in heads]
        pv = [jnp.dot(p[h].astype(BF16), v_ref[0, h, pl.ds(start, tq), :], preferred_element_type=F32)
              for h in heads]
        for h in heads:
            l_ref[h] = alpha[h] * l_ref[h] + pv[h][:, ML_V:]
            acc_ref[h] = alpha[h] * acc_ref[h] + pv[h][:, :ML_V]
            m_ref[h] = m_new[h]

    def body(j, carry):
        step(j, False)
        return carry

    lax.fori_loop(0, qi, body, 0)
    step(qi, True)
    for h in heads:
        o_ref[0, :, h * ML_V:(h + 1) * ML_V] = (acc_ref[h] / l_ref[h]).astype(BF16)


FLASH_HEADS = 2


def _flash(q, k, v):
    B, H, T, _ = q.shape
    tq = min(512, T)
    hp = FLASH_HEADS
    return pl.pallas_call(
        _flash_kernel,
        grid=(B, H // hp, T // tq),
        in_specs=[pl.BlockSpec((1, hp, tq, ML_QK), lambda b, h, i: (b, h, i, 0)),
                  pl.BlockSpec((1, hp, T, ML_QK), lambda b, h, i: (b, h, 0, 0)),
                  pl.BlockSpec((1, hp, T, 2 * ML_V), lambda b, h, i: (b, h, 0, 0))],
        out_specs=pl.BlockSpec((1, tq, hp * ML_V), lambda b, h, i: (b, i, h)),
        out_shape=jax.ShapeDtypeStruct((B, T, H * ML_V), BF16),
        scratch_shapes=[pltpu.VMEM((hp, tq, LANES), F32), pltpu.VMEM((hp, tq, LANES), F32),
                        pltpu.VMEM((hp, tq, ML_V), F32)],
        compiler_params=_params(("parallel", "parallel", "arbitrary")),
        name="mla_flash",
    )(q, k, v)


def _merge_kernel(ya_ref, yb_ref, yc_ref, ga_ref, gb_ref, gc_ref, wa_ref, wb_ref, wc_ref, o_ref):
    def br(y_ref, g_ref, w_ref):
        return jax.nn.sigmoid(g_ref[...].astype(F32)) * jnp.dot(y_ref[...], w_ref[...],
                                                                 preferred_element_type=F32)
    o_ref[...] = (br(ya_ref, ga_ref, wa_ref) + br(yb_ref, gb_ref, wb_ref)
                  + br(yc_ref, gc_ref, wc_ref)).astype(BF16)


def _merge(ya, yb, yc, proj, wa, wb, wc, T):
    N, W = ya.shape
    D = wa.shape[1]
    tm = min(1024, T)
    tn = 512
    ys = pl.BlockSpec((tm, W), lambda i, j: (i, 0))
    gs = lambda k: pl.BlockSpec((tm, tn), lambda i, j: (i, (OFF_GATES + k * D) // tn + j))
    ws = pl.BlockSpec((W, tn), lambda i, j: (0, j))
    return pl.pallas_call(
        _merge_kernel,
        grid=(N // tm, D // tn),
        in_specs=[ys, ys, ys, gs(0), gs(1), gs(2), ws, ws, ws],
        out_specs=pl.BlockSpec((tm, tn), lambda i, j: (i, j)),
        out_shape=jax.ShapeDtypeStruct((N, D), BF16),
        compiler_params=_params(("parallel", "arbitrary")),
        name="merge",
    )(ya, yb, yc, proj, proj, proj, wa, wb, wc)


def _outproj_kernel(m_ref, w_ref, x_ref, g1_ref, ng_ref, sc_ref, sh_ref, wr_ref, rb_ref,
                    xo_ref, h_ref, e_ref, rk_ref, wt_ref, cnt_ref, carry_ref):
    tm = m_ref.shape[0]

    @pl.when(pl.program_id(0) == 0)
    def _():
        carry_ref[...] = jnp.zeros_like(carry_ref)

    x = x_ref[...] + g1_ref[0] * jnp.dot(m_ref[...], w_ref[...], preferred_element_type=F32)
    xo_ref[...] = x
    y = x * lax.rsqrt(jnp.mean(x * x, axis=-1, keepdims=True) + EPS) * ng_ref[...]
    h = y * (1.0 + sc_ref[0]) + sh_ref[0]
    h_ref[...] = h

    logits = jnp.dot(h, wr_ref[...], preferred_element_type=F32, precision=lax.Precision.HIGHEST)
    lt = logits.T
    sc = [jax.nn.sigmoid(lt[r:r + 1, :]) for r in range(N_EXPERTS)]
    bi = [sc[r] + rb_ref[r:r + 1, :] for r in range(N_EXPERTS)]
    gs = []
    for g in range(N_GROUPS):
        b0, b1, b2, b3 = (bi[j * N_GROUPS + g] for j in range(PER_GROUP))
        hi01, lo01 = jnp.maximum(b0, b1), jnp.minimum(b0, b1)
        hi23, lo23 = jnp.maximum(b2, b3), jnp.minimum(b2, b3)
        gs.append(jnp.maximum(hi01, hi23) + jnp.maximum(jnp.minimum(hi01, hi23), jnp.maximum(lo01, lo23)))
    gmax = jnp.maximum(jnp.maximum(gs[0], gs[1]), jnp.maximum(gs[2], gs[3]))
    best = jnp.where(gs[0] == gmax, 0, jnp.where(gs[1] == gmax, 1, jnp.where(gs[2] == gmax, 2, 3)))

    def pick(vals, idx):
        return jnp.where(idx == 0, vals[0], jnp.where(idx == 1, vals[1], jnp.where(idx == 2, vals[2], vals[3])))

    v = [pick([bi[j * N_GROUPS + g] for g in range(N_GROUPS)], best) for j in range(PER_GROUP)]
    s = [pick([sc[j * N_GROUPS + g] for g in range(N_GROUPS)], best) for j in range(PER_GROUP)]

    def first_max(vals):
        mx = jnp.maximum(jnp.maximum(vals[0], vals[1]), jnp.maximum(vals[2], vals[3]))
        return jnp.where(vals[0] == mx, 0, jnp.where(vals[1] == mx, 1, jnp.where(vals[2] == mx, 2, 3)))

    i1 = first_max(v)
    i2 = first_max([jnp.where(i1 == j, -jnp.inf, v[j]) for j in range(PER_GROUP)])
    s1, s2 = pick(s, i1), pick(s, i2)
    tot = s1 + s2
    e1 = best * PER_GROUP + i1
    e2 = best * PER_GROUP + i2
    e_ref[0:1, :] = e1
    e_ref[1:2, :] = e2

    eio = lax.broadcasted_iota(I32, (N_EXPERTS, tm), 0)
    hit1, hit2 = eio == e1, eio == e2
    cnt = (hit1 | hit2).astype(BF16)
    r = lax.broadcasted_iota(I32, (tm, tm), 0)
    c = lax.broadcasted_iota(I32, (tm, tm), 1)
    before = (r < c).astype(BF16)
    run = jnp.dot(cnt, before, preferred_element_type=F32) + carry_ref[:, 0:1]
    rk_ref[0:1, :] = jnp.sum(jnp.where(hit1, run, 0.0), axis=0, keepdims=True).astype(I32)
    rk_ref[1:2, :] = jnp.sum(jnp.where(hit2, run, 0.0), axis=0, keepdims=True).astype(I32)
    carry_ref[...] = carry_ref[...] + jnp.sum(cnt.astype(F32), axis=1, keepdims=True)
    cnt_ref[...] = carry_ref[...]

    rid = lax.broadcasted_iota(I32, (LANES, tm), 0)
    wrow = jnp.where(rid == 0, s1 / tot, jnp.where(rid == 1, s2 / tot, 0.0))
    wt_ref[...] = wrow.T[:, 0:8]


def _outproj(merged, w_out, x2, mod, norm_g, wr, rb, T):
    N, D = x2.shape
    tm = min(512, T)
    tpb = T // tm
    mspec = lambda k: pl.BlockSpec((1, 1, D), lambda i: ((i // tpb) * N_MOD + k, 0, 0))
    return pl.pallas_call(
        _outproj_kernel,
        grid=(N // tm,),
        in_specs=[pl.BlockSpec((tm, D), lambda i: (i, 0)),
                  pl.BlockSpec((D, D), lambda i: (0, 0), pipeline_mode=pl.Buffered(1)),
                  pl.BlockSpec((tm, D), lambda i: (i, 0)),
                  mspec(2),
                  pl.BlockSpec((1, D), lambda i: (0, 0)),
                  mspec(4), mspec(3),
                  pl.BlockSpec((D, LANES), lambda i: (0, 0)),
                  pl.BlockSpec((N_EXPERTS, 1), lambda i: (0, 0))],
        out_specs=[pl.BlockSpec((tm, D), lambda i: (i, 0)),
                   pl.BlockSpec((tm, D), lambda i: (i, 0)),
                   pl.BlockSpec((2, tm), lambda i: (0, i)),
                   pl.BlockSpec((2, tm), lambda i: (0, i)),
                   pl.BlockSpec((tm, 8), lambda i: (i, 0)),
                   pl.BlockSpec((N_EXPERTS, LANES), lambda i: (0, 0))],
        out_shape=[jax.ShapeDtypeStruct((N, D), F32),
                   jax.ShapeDtypeStruct((N, D), F32),
                   jax.ShapeDtypeStruct((2, N), I32),
                   jax.ShapeDtypeStruct((2, N), I32),
                   jax.ShapeDtypeStruct((N, 8), F32),
                   jax.ShapeDtypeStruct((N_EXPERTS, LANES), F32)],
        scratch_shapes=[pltpu.VMEM((N_EXPERTS, LANES), F32)],
        compiler_params=_params(("arbitrary",)),
        name="outproj_router",
    )(merged, w_out, x2, mod, norm_g.reshape(1, D), mod, mod, wr, rb)


def _dispatch_kernel(dest_ref, h_ref, xs_in_ref, xs_ref, sem):
    del xs_in_ref
    tm = h_ref.shape[0]

    def row_copy(t, k):
        return pltpu.make_async_copy(h_ref.at[pl.ds(t, 1), :], xs_ref.at[pl.ds(dest_ref[k, t], 1), :], sem)

    def issue(t, carry):
        row_copy(t, 0).start()
        row_copy(t, 1).start()
        return carry

    def drain(t, carry):
        row_copy(t, 0).wait()
        row_copy(t, 1).wait()
        return carry

    lax.fori_loop(0, tm, issue, 0)
    lax.fori_loop(0, tm, drain, 0)


def _dispatch(dest, h, n_rows, T):
    N, D = h.shape
    tm = min(512, T)
    xs0 = jnp.zeros((n_rows, D), F32)
    return pl.pallas_call(
        _dispatch_kernel,
        grid=(N // tm,),
        in_specs=[pl.BlockSpec((2, tm), lambda i: (0, i), memory_space=pltpu.SMEM),
                  pl.BlockSpec((tm, D), lambda i: (i, 0)),
                  pl.BlockSpec(memory_space=pl.ANY)],
        out_specs=pl.BlockSpec(memory_space=pl.ANY),
        out_shape=jax.ShapeDtypeStruct((n_rows, D), F32),
        scratch_shapes=[pltpu.SemaphoreType.DMA(())],
        input_output_aliases={2: 0},
        compiler_params=_params(("arbitrary",)),
        name="moe_dispatch",
    )(dest, h, xs0)


def _experts_kernel(be_ref, nb_ref, x_ref, wg_ref, wu_ref, wd_ref, y_ref):
    @pl.when(pl.program_id(0) < nb_ref[0])
    def _():
        x = x_ref[...].astype(BF16)
        a = jnp.dot(x, wg_ref[0], preferred_element_type=F32)
        b = jnp.dot(x, wu_ref[0], preferred_element_type=F32)
        hmid = (a * jax.nn.sigmoid(a) * b).astype(BF16)
        y_ref[...] = jnp.dot(hmid, wd_ref[0], preferred_element_type=F32)

    @pl.when(pl.program_id(0) >= nb_ref[0])
    def _():
        y_ref[...] = jnp.zeros_like(y_ref)


def _experts(blk_e, n_used, xs, wg, wu, wd):
    R, D = xs.shape
    F = wg.shape[2]
    nb = R // MOE_ROWS
    row = lambda i, be, nu: (jnp.minimum(i, nu[0] - 1), 0)
    return pl.pallas_call(
        _experts_kernel,
        grid_spec=pltpu.PrefetchScalarGridSpec(
            num_scalar_prefetch=2,
            grid=(nb,),
            in_specs=[pl.BlockSpec((MOE_ROWS, D), row),
                      pl.BlockSpec((1, D, F), lambda i, be, nu: (be[i], 0, 0)),
                      pl.BlockSpec((1, D, F), lambda i, be, nu: (be[i], 0, 0)),
                      pl.BlockSpec((1, F, D), lambda i, be, nu: (be[i], 0, 0))],
            out_specs=pl.BlockSpec((MOE_ROWS, D), lambda i, be, nu: (i, 0))),
        out_shape=jax.ShapeDtypeStruct((R, D), F32),
        compiler_params=_params(("arbitrary",)),
        name="moe_experts",
    )(blk_e, n_used, xs, wg, wu, wd)


def _combine_kernel(dest_ref, y_ref, wt_ref, x_ref, g2_ref, fg_ref, o_ref, buf_ref, sem, *, final):
    tm = x_ref.shape[0]

    def row_copy(t, k):
        return pltpu.make_async_copy(y_ref.at[pl.ds(dest_ref[k, t], 1), :], buf_ref.at[k, pl.ds(t, 1), :], sem)

    def issue(t, carry):
        row_copy(t, 0).start()
        row_copy(t, 1).start()
        return carry

    def drain(t, carry):
        row_copy(t, 0).wait()
        row_copy(t, 1).wait()
        return carry

    lax.fori_loop(0, tm, issue, 0)
    lax.fori_loop(0, tm, drain, 0)
    wt = wt_ref[...]
    moe = buf_ref[0] * wt[:, 0:1] + buf_ref[1] * wt[:, 1:2]
    x = x_ref[...] + g2_ref[0] * moe
    if final:
        x = x * lax.rsqrt(jnp.mean(x * x, axis=-1, keepdims=True) + EPS) * fg_ref[...]
    o_ref[...] = x


def _combine(dest, y, wt, x2, mod, final_g, T, final):
    N, D = x2.shape
    tm = min(256, T)
    tpb = T // tm
    return pl.pallas_call(
        functools.partial(_combine_kernel, final=final),
        grid=(N // tm,),
        in_specs=[pl.BlockSpec((2, tm), lambda i: (0, i), memory_space=pltpu.SMEM),
                  pl.BlockSpec(memory_space=pl.ANY),
                  pl.BlockSpec((tm, 8), lambda i: (i, 0)),
                  pl.BlockSpec((tm, D), lambda i: (i, 0)),
                  pl.BlockSpec((1, 1, D), lambda i: ((i // tpb) * N_MOD + 5, 0, 0)),
                  pl.BlockSpec((1, D), lambda i: (0, 0))],
        out_specs=pl.BlockSpec((tm, D), lambda i: (i, 0)),
        out_shape=jax.ShapeDtypeStruct((N, D), F32),
        scratch_shapes=[pltpu.VMEM((2, tm, D), F32), pltpu.SemaphoreType.DMA(())],
        compiler_params=_params(("arbitrary",)),
        name="moe_combine",
    )(dest, y, wt, x2, mod, final_g.reshape(1, D))


def _pack_w_in(w):
    D = w.shape[0]
    w = w.astype(BF16)
    widths = (GM_WIDTH, GM_WIDTH, 3 * DN_WIDTH, DN_WIDTH, DN_HEADS, DN_HEADS, ML_RANK, ML_RANK, ML_ROPE,
              3 * D)
    offs = [0]
    for wd in widths:
        offs.append(offs[-1] + wd)
    u, v, qkv, z, b, a, cq, ckv, kr, gates = (w[:, offs[i]:offs[i + 1]] for i in range(len(widths)))
    kr_rot = jnp.concatenate([-kr[:, ML_ROPE // 2:], kr[:, :ML_ROPE // 2]], axis=1)
    zeros = lambda n: jnp.zeros((D, n), BF16)
    packed = jnp.concatenate([u, v, qkv, z, gates, cq, ckv, kr, kr_rot, b, a,
                              zeros(LANES - 2 * DN_HEADS), zeros(PROJ_COLS - OFF_BA - LANES)], axis=1)
    return packed


def _block_table(counts, n_blocks):
    per = (counts + MOE_ROWS - 1) // MOE_ROWS
    ends = jnp.cumsum(per)
    first_row = (ends - per) * MOE_ROWS
    blk_e = jnp.minimum(jnp.searchsorted(ends, jnp.arange(n_blocks, dtype=I32), side="right"),
                        N_EXPERTS - 1).astype(I32)
    return blk_e, ends[-1:].astype(I32), first_row.astype(I32)


def kernel(x, c, positions, w_ada, b_ada, norm1_g, norm2_g, w_in, gmlp_ln_g, gmlp_ln_b, gmlp_ws, gmlp_bs, dn_conv_w, dn_a_log, dn_dt_bias, dn_norm_g, mla_cq_g, mla_ckv_g, mla_w_uq, mla_w_ukv, w_br_gmlp, w_br_dn, w_br_mla, w_out, w_router, router_bias, w_gate, w_up, w_down, final_norm_g):
    B, T, D = x.shape
    L = w_in.shape[0]
    N = B * T
    x2 = x.reshape(N, D)
    pos_col = positions.reshape(N, 1)
    mod_all = _ada_mod(c, w_ada, b_ada).reshape(L, B * N_MOD, 1, D)

    perm = jnp.array([(r % N_GROUPS) * PER_GROUP + r // N_GROUPS for r in range(N_EXPERTS)], I32)
    wr = jnp.concatenate([w_router[:, perm], jnp.zeros((D, LANES - N_EXPERTS), F32)], axis=1)
    rb = router_bias[perm].reshape(N_EXPERTS, 1)
    n_blocks = (2 * N) // MOE_ROWS + N_EXPERTS

    for l in range(L):
        mod = mod_all[l]
        proj = _inproj(x2, norm1_g[l], mod, _pack_w_in(w_in[l]), T)
        ya = _gmlp(proj, gmlp_ln_g[l], gmlp_ln_b[l], gmlp_ws[l], gmlp_bs[l], T)
        yb = _deltanet(proj, dn_conv_w[l], dn_a_log[l], dn_dt_bias[l], dn_norm_g[l], B, T)
        q, k, v = _mla_proj(proj, pos_col, mla_cq_g[l], mla_ckv_g[l], mla_w_uq[l], mla_w_ukv[l], B, T)
        yc = _flash(q, k, v).reshape(N, ML_HEADS * ML_V)
        merged = _merge(ya, yb, yc, proj, w_br_gmlp[l].astype(BF16), w_br_dn[l].astype(BF16),
                        w_br_mla[l].astype(BF16), T)
        x2, h2, e, rank, wt, counts = _outproj(merged, w_out[l].astype(BF16), x2, mod, norm2_g[l], wr, rb, T)
        blk_e, n_used, first_row = _block_table(counts[:, 0].astype(I32), n_blocks)
        onehot = e[:, :, None] == jnp.arange(N_EXPERTS, dtype=I32)
        dest = rank + jnp.sum(jnp.where(onehot, first_row, 0), axis=-1)
        xs = _dispatch(dest, h2, n_blocks * MOE_ROWS, T)
        y = _experts(blk_e, n_used, xs, w_gate[l].astype(BF16), w_up[l].astype(BF16), w_down[l].astype(BF16))
        x2 = _combine(dest, y, wt, x2, mod, final_norm_g, T, final=(l == L - 1))
    return x2.reshape(B, T, D)
```

```python
import functools
import math

import jax
import jax.numpy as jnp
from jax import lax
from jax.experimental import pallas as pl
from jax.experimental.pallas import tpu as pltpu

F32 = jnp.float32
BF16 = jnp.bfloat16
I32 = jnp.int32

EPS = 1e-6
N_MOD = 6
GM_GROUPS, GM_DIM, GM_CHUNK = 8, 128, 128
GM_WIDTH = GM_GROUPS * GM_DIM
DN_HEADS, DN_DIM, DN_CONV = 8, 128, 4
DN_WIDTH = DN_HEADS * DN_DIM
DN_CHUNK = 128
DN_BASE = 16
ML_HEADS, ML_RANK, ML_NOPE, ML_ROPE, ML_V = 8, 512, 128, 64, 128
ML_QK = ML_NOPE + ML_ROPE
ROPE_THETA = 10000.0
N_EXPERTS, N_GROUPS, PER_GROUP, D_EXPERT = 16, 4, 4, 1024
MOE_ROWS = 256

LANES = 128
OFF_U, OFF_V, OFF_Q, OFF_K, OFF_VV, OFF_Z = 0, 1024, 2048, 3072, 4096, 5120
OFF_GATES = 6144
OFF_CQ = OFF_GATES + 3 * 2048
OFF_CKV = OFF_CQ + ML_RANK
OFF_KR = OFF_CKV + ML_RANK
OFF_BA = OFF_KR + LANES
PROJ_TN = 1536
PROJ_COLS = 13824

VMEM_LIMIT = 52 * 1024 * 1024


def _params(sem, vmem=VMEM_LIMIT):
    return pltpu.CompilerParams(dimension_semantics=sem, vmem_limit_bytes=vmem)


def _ada_kernel(c_ref, w_ref, b_ref, o_ref):
    c = c_ref[...]
    ca = (c * jax.nn.sigmoid(c)).astype(BF16)
    o_ref[0] = jnp.dot(ca, w_ref[0].astype(BF16), preferred_element_type=F32) + b_ref[0]


def _ada_mod(c, w_ada, b_ada):
    L, D, M = w_ada.shape
    B = c.shape[0]
    tn = 1024
    return pl.pallas_call(
        _ada_kernel,
        grid=(L, M // tn),
        in_specs=[pl.BlockSpec((B, D), lambda l, j: (0, 0)),
                  pl.BlockSpec((1, D, tn), lambda l, j: (l, 0, j)),
                  pl.BlockSpec((1, 1, tn), lambda l, j: (l, 0, j))],
        out_specs=pl.BlockSpec((1, B, tn), lambda l, j: (l, 0, j)),
        out_shape=jax.ShapeDtypeStruct((L, B, M), F32),
        compiler_params=_params(("parallel", "parallel")),
        name="ada_mod",
    )(c, w_ada, b_ada.reshape(L, 1, M))


def _inproj_kernel(x_ref, g_ref, sc_ref, sh_ref, w_ref, o_ref, h_ref):
    @pl.when(pl.program_id(1) == 0)
    def _():
        x = x_ref[...]
        y = x * lax.rsqrt(jnp.mean(x * x, axis=-1, keepdims=True) + EPS) * g_ref[...]
        h_ref[...] = (y * (1.0 + sc_ref[0]) + sh_ref[0]).astype(BF16)

    o_ref[...] = jnp.dot(h_ref[...], w_ref[0], preferred_element_type=F32).astype(BF16)


def _inproj(x2, g, mod, w, l, T):
    N, D = x2.shape
    C = w.shape[2]
    tm = min(1024, T)
    tpb = T // tm
    return pl.pallas_call(
        _inproj_kernel,
        grid=(N // tm, C // PROJ_TN),
        in_specs=[pl.BlockSpec((tm, D), lambda i, j: (i, 0)),
                  pl.BlockSpec((1, D), lambda i, j: (0, 0)),
                  pl.BlockSpec((1, 1, D), lambda i, j: ((i // tpb) * N_MOD + 1, 0, 0)),
                  pl.BlockSpec((1, 1, D), lambda i, j: ((i // tpb) * N_MOD + 0, 0, 0)),
                  pl.BlockSpec((1, D, PROJ_TN), lambda i, j: (l, 0, j))],
        out_specs=pl.BlockSpec((tm, PROJ_TN), lambda i, j: (i, j)),
        out_shape=jax.ShapeDtypeStruct((N, C), BF16),
        scratch_shapes=[pltpu.VMEM((tm, D), BF16)],
        compiler_params=_params(("parallel", "arbitrary")),
        name="inproj",
    )(x2, g.reshape(1, D), mod, mod, w)


def _gmlp_kernel(u_ref, v_ref, lg_ref, lb_ref, ws_ref, bs_ref, o_ref):
    tm = u_ref.shape[0]
    v = jax.nn.gelu(v_ref[...].astype(F32))
    mu = jnp.mean(v, axis=-1, keepdims=True)
    vc = v - mu
    var = jnp.mean(vc * vc, axis=-1, keepdims=True)
    vn = (vc * lax.rsqrt(var + EPS) * lg_ref[...] + lb_ref[...]).astype(BF16)
    row = lax.broadcasted_iota(I32, (GM_CHUNK, GM_CHUNK), 0)
    col = lax.broadcasted_iota(I32, (GM_CHUNK, GM_CHUNK), 1)
    causal = col <= row
    for g in range(GM_GROUPS):
        w = jnp.where(causal, ws_ref[g], 0.0).astype(BF16)
        lo, hi = g * GM_DIM, (g + 1) * GM_DIM
        for c in range(tm // GM_CHUNK):
            r0, r1 = c * GM_CHUNK, (c + 1) * GM_CHUNK
            s = jnp.dot(w, vn[r0:r1, lo:hi], preferred_element_type=F32) + bs_ref[g]
            u = jax.nn.gelu(u_ref[r0:r1, lo:hi].astype(F32))
            o_ref[r0:r1, lo:hi] = (u * s).astype(BF16)


def _gmlp(proj, ln_g, ln_b, ws, bs, T):
    N = proj.shape[0]
    tm = min(512, T)
    W = GM_WIDTH
    return pl.pallas_call(
        _gmlp_kernel,
        grid=(N // tm,),
        in_specs=[pl.BlockSpec((tm, W), lambda i: (i, OFF_U // W)),
                  pl.BlockSpec((tm, W), lambda i: (i, OFF_V // W)),
                  pl.BlockSpec((1, W), lambda i: (0, 0)),
                  pl.BlockSpec((1, W), lambda i: (0, 0)),
                  pl.BlockSpec((GM_GROUPS, GM_CHUNK, GM_CHUNK), lambda i: (0, 0, 0)),
                  pl.BlockSpec((GM_GROUPS, GM_CHUNK, 1), lambda i: (0, 0, 0))],
        out_specs=pl.BlockSpec((tm, W), lambda i: (i, 0)),
        out_shape=jax.ShapeDtypeStruct((N, W), BF16),
        compiler_params=_params(("parallel",)),
        name="gmlp",
    )(proj, proj, ln_g.reshape(1, W), ln_b.reshape(1, W), ws, bs.reshape(GM_GROUPS, GM_CHUNK, 1))


def _mm(a, b):
    return jnp.dot(a.astype(BF16), b.astype(BF16), preferred_element_type=F32)


def _mm_nt(a, b):
    return lax.dot_general(a.astype(BF16), b.astype(BF16), (((1,), (1,)), ((), ())),
                           preferred_element_type=F32)


def _mm_tn(a, b):
    return lax.dot_general(a.astype(BF16), b.astype(BF16), (((0,), (0,)), ((), ())),
                           preferred_element_type=F32)


def _unit_lower_inverses(lows):
    C = lows[0].shape[0]
    n = range(len(lows))
    r = lax.broadcasted_iota(I32, (C, C), 0)
    c = lax.broadcasted_iota(I32, (C, C), 1)
    eye = (r == c).astype(F32)
    s = DN_BASE
    diag = (r // s) == (c // s)
    p = [-jnp.where(diag, lows[i], 0.0) for i in n]
    t = [eye + p[i] for i in n]
    for _ in range(int(math.log2(s)) - 1):
        p = [_mm(p[i], p[i]) for i in n]
        tp = [_mm(t[i], p[i]) for i in n]
        t = [t[i] + tp[i] for i in n]
    while s < C:
        band = ((r // (2 * s)) == (c // (2 * s))) & ((r // s) != (c // s))
        ot = [_mm(jnp.where(band, lows[i], 0.0), t[i]) for i in n]
        tot = [_mm(t[i], ot[i]) for i in n]
        t = [t[i] - tot[i] for i in n]
        s *= 2
    return t


def _dn_kernel(q_ref, k_ref, v_ref, z_ref, ba_ref, cw_ref, alog_ref, dtb_ref, ng_ref, o_ref,
               ext_ref, st_ref):
    tc = q_ref.shape[0]
    C = DN_CHUNK

    @pl.when(pl.program_id(1) == 0)
    def _():
        ext_ref[:, 0:8, :] = jnp.zeros((3, 8, DN_WIDTH), F32)
        st_ref[...] = jnp.zeros_like(st_ref)

    @pl.when(pl.program_id(1) != 0)
    def _():
        ext_ref[:, 0:8, :] = ext_ref[:, tc:tc + 8, :]

    ext_ref[0, 8:, :] = q_ref[...].astype(F32)
    ext_ref[1, 8:, :] = k_ref[...].astype(F32)
    ext_ref[2, 8:, :] = v_ref[...].astype(F32)

    def conv_silu(i):
        acc = jnp.zeros((tc, DN_WIDTH), F32)
        for kk in range(DN_CONV):
            acc = acc + cw_ref[i, kk:kk + 1, :] * ext_ref[i, pl.ds(8 - (DN_CONV - 1) + kk, tc), :]
        return acc * jax.nn.sigmoid(acc)

    qa, ka, va = conv_silu(0), conv_silu(1), conv_silu(2)

    ba = ba_ref[...].astype(F32)
    beta_all = jax.nn.sigmoid(ba)
    g_all = -jnp.exp(alog_ref[...]) * jax.nn.softplus(ba + dtb_ref[...])
    pos = lax.broadcasted_iota(I32, (tc, LANES), 0) % C
    gc_all = g_all
    sh = 1
    while sh < C:
        gc_all = gc_all + jnp.where(pos >= sh, pltpu.roll(gc_all, sh, 0), 0.0)
        sh *= 2
    gc_rows = gc_all.T

    r = lax.broadcasted_iota(I32, (C, C), 0)
    c = lax.broadcasted_iota(I32, (C, C), 1)
    incl = c <= r
    strict = c < r

    H = DN_HEADS
    nc = tc // C
    idx = [(ci, h) for ci in range(nc) for h in range(H)]
    G = range(len(idx))

    def piece(x, ci, h):
        return x[ci * C:(ci + 1) * C, h * DN_DIM:(h + 1) * DN_DIM]

    def l2n(x):
        return x * lax.rsqrt(jnp.sum(x * x, axis=-1, keepdims=True) + EPS)

    q = [l2n(piece(qa, ci, h)) * (DN_DIM ** -0.5) for ci, h in idx]
    k = [l2n(piece(ka, ci, h)) for ci, h in idx]
    v = [piece(va, ci, h) for ci, h in idx]
    beta = [beta_all[ci * C:(ci + 1) * C, h:h + 1] for ci, h in idx]
    g_c = [gc_all[ci * C:(ci + 1) * C, H + h:H + h + 1] for ci, h in idx]
    g_r = [gc_rows[H + h:H + h + 1, ci * C:(ci + 1) * C] for ci, h in idx]
    g_last = [g_r[i][:, C - 1:C] for i in G]
    decay = [jnp.exp(jnp.where(incl, g_c[i] - g_r[i], -1e30)) for i in G]
    kb = [k[i] * beta[i] for i in G]
    kk = [_mm_nt(kb[i], k[i]) for i in G]
    qk = [_mm_nt(q[i], k[i]) for i in G]
    tinv = _unit_lower_inverses([jnp.where(strict, kk[i] * decay[i], 0.0) for i in G])
    intra = [jnp.where(incl, qk[i] * decay[i], 0.0) for i in G]
    eg = [jnp.exp(g_c[i]) for i in G]
    u = [_mm(tinv[i], v[i] * beta[i]) for i in G]
    w = [_mm(tinv[i], kb[i] * eg[i]) for i in G]
    qg = [q[i] * eg[i] for i in G]
    k_dec = [k[i] * jnp.exp(g_last[i] - g_c[i]) for i in G]

    state = [st_ref[h] for h in range(H)]
    for ci in range(nc):
        ids = [ci * H + h for h in range(H)]
        ws = [_mm(w[i], state[h]) for h, i in enumerate(ids)]
        qs = [_mm(qg[i], state[h]) for h, i in enumerate(ids)]
        v_new = [u[i] - ws[h] for h, i in enumerate(ids)]
        av = [_mm(intra[i], v_new[h]) for h, i in enumerate(ids)]
        kv = [_mm_tn(k_dec[i], v_new[h]) for h, i in enumerate(ids)]
        state = [state[h] * jnp.exp(g_last[i]) + kv[h] for h, i in enumerate(ids)]
        for h in range(H):
            o_c = qs[h] + av[h]
            o_n = o_c * lax.rsqrt(jnp.mean(o_c * o_c, axis=-1, keepdims=True) + EPS) * ng_ref[...]
            z = piece(z_ref, ci, h).astype(F32)
            o_ref[ci * C:(ci + 1) * C, h * DN_DIM:(h + 1) * DN_DIM] = (o_n * (z * jax.nn.sigmoid(z))).astype(BF16)
    for h in range(H):
        st_ref[h] = state[h]


def _deltanet(proj, conv_w, a_log, dt_bias, norm_g, B, T):
    N = proj.shape[0]
    tc = min(256, T)
    tpb = T // tc
    W = DN_WIDTH
    cw = conv_w.reshape(DN_CONV, 3, W).transpose(1, 0, 2)
    pad = jnp.zeros((LANES - 2 * DN_HEADS,), F32)
    alog = jnp.concatenate([jnp.zeros((DN_HEADS,), F32), a_log, pad]).reshape(1, LANES)
    dtb = jnp.concatenate([jnp.zeros((DN_HEADS,), F32), dt_bias, pad]).reshape(1, LANES)
    blk = lambda off: pl.BlockSpec((tc, W), lambda b, t: (b * tpb + t, off // W))
    return pl.pallas_call(
        _dn_kernel,
        grid=(B, tpb),
        in_specs=[blk(OFF_Q), blk(OFF_K), blk(OFF_VV), blk(OFF_Z),
                  pl.BlockSpec((tc, LANES), lambda b, t: (b * tpb + t, OFF_BA // LANES)),
                  pl.BlockSpec((3, DN_CONV, W), lambda b, t: (0, 0, 0)),
                  pl.BlockSpec((1, LANES), lambda b, t: (0, 0)),
                  pl.BlockSpec((1, LANES), lambda b, t: (0, 0)),
                  pl.BlockSpec((1, DN_DIM), lambda b, t: (0, 0))],
        out_specs=pl.BlockSpec((tc, W), lambda b, t: (b * tpb + t, 0)),
        out_shape=jax.ShapeDtypeStruct((N, W), BF16),
        scratch_shapes=[pltpu.VMEM((3, tc + 8, W), F32),
                        pltpu.VMEM((DN_HEADS, DN_DIM, DN_DIM), F32)],
        compiler_params=_params(("parallel", "arbitrary")),
        name="deltanet",
    )(proj, proj, proj, proj, proj, cw, alog, dtb, norm_g.reshape(1, DN_DIM))


def _rope_kernel(pos_ref, invf_ref, o_ref):
    ang = pos_ref[...].astype(F32) * invf_ref[...]
    o_ref[0] = jnp.cos(ang)
    o_ref[1] = jnp.sin(ang)


def _rope_tables(pos_col, T):
    N = pos_col.shape[0]
    tm = min(1024, T)
    half = ROPE_THETA ** (-jnp.arange(0, ML_ROPE, 2, dtype=F32) / ML_ROPE)
    invf = jnp.tile(half, LANES // (ML_ROPE // 2)).reshape(1, LANES)
    return pl.pallas_call(
        _rope_kernel,
        grid=(N // tm,),
        in_specs=[pl.BlockSpec((tm, 1), lambda i: (i, 0)),
                  pl.BlockSpec((1, LANES), lambda i: (0, 0))],
        out_specs=pl.BlockSpec((2, tm, LANES), lambda i: (0, i, 0)),
        out_shape=jax.ShapeDtypeStruct((2, N, LANES), F32),
        compiler_params=_params(("parallel",)),
        name="rope_tables",
    )(pos_col, invf)


def _mla_proj_kernel(cq_ref, ckv_ref, kr_ref, cs_ref, gq_ref, gkv_ref, wq_ref, wkv_ref,
                     q_ref, k_ref, v_ref):
    tm = cq_ref.shape[0]
    cos, sin = cs_ref[0], cs_ref[1]
    scale = ML_QK ** -0.5 * math.log2(math.e)

    def rms(x, g):
        return (x * lax.rsqrt(jnp.mean(x * x, axis=-1, keepdims=True) + EPS) * g).astype(BF16)

    def rope(pair):
        return pair * cos + pltpu.roll(pair, ML_ROPE, 1) * sin

    cqn = rms(cq_ref[...].astype(F32), gq_ref[...])
    ckvn = rms(ckv_ref[...].astype(F32), gkv_ref[...])
    k_rope = rope(kr_ref[...].astype(F32))[:, 0:ML_ROPE].astype(BF16)
    ones = jnp.ones((tm, ML_V), BF16)
    W2 = 2 * LANES
    for h in range(ML_HEADS):
        qa = jnp.dot(cqn, wq_ref[0, :, h * W2:(h + 1) * W2], preferred_element_type=F32)
        q_ref[0, h, :, 0:ML_NOPE] = (qa[:, 0:ML_NOPE] * scale).astype(BF16)
        q_ref[0, h, :, ML_NOPE:ML_QK] = (rope(qa[:, ML_NOPE:]) * scale)[:, 0:ML_ROPE].astype(BF16)
        kva = jnp.dot(ckvn, wkv_ref[0, :, h * W2:(h + 1) * W2], preferred_element_type=F32)
        k_ref[0, h, :, 0:ML_NOPE] = kva[:, 0:ML_NOPE].astype(BF16)
        k_ref[0, h, :, ML_NOPE:ML_QK] = k_rope
        v_ref[0, h, :, 0:ML_V] = kva[:, ML_NOPE:].astype(BF16)
        v_ref[0, h, :, ML_V:] = ones


def _pack_mla_weights(w_uq, w_ukv):
    L = w_uq.shape[0]
    H, R = ML_HEADS, ML_RANK
    wq = w_uq.reshape(L, R, H, ML_QK)
    wr = wq[..., ML_NOPE:]
    wq = jnp.concatenate([wq, -wr[..., ML_ROPE // 2:], wr[..., :ML_ROPE // 2]], axis=-1)
    return wq.reshape(L, R, H * 2 * LANES).astype(BF16), w_ukv.astype(BF16)


def _mla_proj(proj, rope_cs, cq_g, ckv_g, wq, wkv, l, B, T):
    N = proj.shape[0]
    H, R = ML_HEADS, ML_RANK
    tm = min(512, T)
    tpb = T // tm
    hs = lambda d: pl.BlockSpec((1, H, tm, d), lambda i: (i // tpb, 0, i % tpb, 0))
    ws = pl.BlockSpec((1, R, H * 2 * LANES), lambda i: (l, 0, 0))
    return pl.pallas_call(
        _mla_proj_kernel,
        grid=(N // tm,),
        in_specs=[pl.BlockSpec((tm, R), lambda i: (i, OFF_CQ // R)),
                  pl.BlockSpec((tm, R), lambda i: (i, OFF_CKV // R)),
                  pl.BlockSpec((tm, LANES), lambda i: (i, OFF_KR // LANES)),
                  pl.BlockSpec((2, tm, LANES), lambda i: (0, i, 0)),
                  pl.BlockSpec((1, R), lambda i: (0, 0)),
                  pl.BlockSpec((1, R), lambda i: (0, 0)),
                  ws, ws],
        out_specs=[hs(ML_QK), hs(ML_QK), hs(2 * ML_V)],
        out_shape=[jax.ShapeDtypeStruct((B, H, T, ML_QK), BF16),
                   jax.ShapeDtypeStruct((B, H, T, ML_QK), BF16),
                   jax.ShapeDtypeStruct((B, H, T, 2 * ML_V), BF16)],
        compiler_params=_params(("parallel",)),
        name="mla_proj",
    )(proj, proj, proj, rope_cs, cq_g.reshape(1, R), ckv_g.reshape(1, R), wq, wkv)


def _flash_kernel(q_ref, k_ref, v_ref, o_ref, m_ref, l_ref, acc_ref):
    hp, tq = q_ref.shape[1], q_ref.shape[2]
    heads = range(hp)
    qi = pl.program_id(2)
    m_ref[...] = jnp.full_like(m_ref, -jnp.inf)
    l_ref[...] = jnp.zeros_like(l_ref)
    acc_ref[...] = jnp.zeros_like(acc_ref)

    def step(j, masked):
        start = pl.multiple_of(j * tq, tq)
        s = [lax.dot_general(q_ref[0, h], k_ref[0, h, pl.ds(start, tq), :], (((1,), (1,)), ((), ())),
                             preferred_element_type=F32) for h in heads]
        if masked:
            r = lax.broadcasted_iota(I32, (tq, tq), 0)
            c = lax.broadcasted_iota(I32, (tq, tq), 1)
            s = [jnp.where(c <= r, s[h], -jnp.inf) for h in heads]
        m_prev = [m_ref[h] for h in heads]
        m_new = [jnp.maximum(m_prev[h], jnp.max(s[h], axis=-1, keepdims=True)) for h in heads]
        p = [jnp.exp2(s[h] - jnp.concatenate([m_new[h]] * (tq // LANES), axis=1)) for h in heads]
        alpha = [jnp.exp2(m_prev[h] - m_new[h]) for h in heads]
        pv = [jnp.dot(p[h].astype(BF16), v_ref[0, h, pl.ds(start, tq), :], preferred_element_type=F32)
              for h in heads]
        for h in heads:
            l_ref[h] = alpha[h] * l_ref[h] + pv[h][:, ML_V:]
            acc_ref[h] = alpha[h] * acc_ref[h] + pv[h][:, :ML_V]
            m_ref[h] = m_new[h]

    def body(j, carry):
        step(j, False)
        return carry

    lax.fori_loop(0, qi, body, 0)
    step(qi, True)
    for h in heads:
        o_ref[0, :, h * ML_V:(h + 1) * ML_V] = (acc_ref[h] / l_ref[h]).astype(BF16)


FLASH_HEADS = 2


def _flash(q, k, v):
    B, H, T, _ = q.shape
    tq = min(512, T)
    hp = FLASH_HEADS
    return pl.pallas_call(
        _flash_kernel,
        grid=(B, H // hp, T // tq),
        in_specs=[pl.BlockSpec((1, hp, tq, ML_QK), lambda b, h, i: (b, h, i, 0)),
                  pl.BlockSpec((1, hp, T, ML_QK), lambda b, h, i: (b, h, 0, 0)),
                  pl.BlockSpec((1, hp, T, 2 * ML_V), lambda b, h, i: (b, h, 0, 0))],
        out_specs=pl.BlockSpec((1, tq, hp * ML_V), lambda b, h, i: (b, i, h)),
        out_shape=jax.ShapeDtypeStruct((B, T, H * ML_V), BF16),
        scratch_shapes=[pltpu.VMEM((hp, tq, LANES), F32), pltpu.VMEM((hp, tq, LANES), F32),
                        pltpu.VMEM((hp, tq, ML_V), F32)],
        compiler_params=_params(("parallel", "parallel", "arbitrary")),
        name="mla_flash",
    )(q, k, v)


def _merge_kernel(ya_ref, yb_ref, yc_ref, ga_ref, gb_ref, gc_ref, wa_ref, wb_ref, wc_ref, o_ref):
    def br(y_ref, g_ref, w_ref):
        return jax.nn.sigmoid(g_ref[...].astype(F32)) * jnp.dot(y_ref[...], w_ref[0],
                                                                 preferred_element_type=F32)
    o_ref[...] = (br(ya_ref, ga_ref, wa_ref) + br(yb_ref, gb_ref, wb_ref)
                  + br(yc_ref, gc_ref, wc_ref)).astype(BF16)


def _merge(ya, yb, yc, proj, wa, wb, wc, l, T):
    N, W = ya.shape
    D = wa.shape[2]
    tm = min(1024, T)
    tn = 512
    ys = pl.BlockSpec((tm, W), lambda i, j: (i, 0))
    gs = lambda k: pl.BlockSpec((tm, tn), lambda i, j: (i, (OFF_GATES + k * D) // tn + j))
    ws = pl.BlockSpec((1, W, tn), lambda i, j: (l, 0, j))
    return pl.pallas_call(
        _merge_kernel,
        grid=(N // tm, D // tn),
        in_specs=[ys, ys, ys, gs(0), gs(1), gs(2), ws, ws, ws],
        out_specs=pl.BlockSpec((tm, tn), lambda i, j: (i, j)),
        out_shape=jax.ShapeDtypeStruct((N, D), BF16),
        compiler_params=_params(("parallel", "arbitrary")),
        name="merge",
    )(ya, yb, yc, proj, proj, proj, wa, wb, wc)


def _outproj_kernel(m_ref, w_ref, x_ref, g1_ref, ng_ref, sc_ref, sh_ref, wr_ref, rb_ref,
                    xo_ref, h_ref, e_ref, rk_ref, wt_ref, cnt_ref, carry_ref):
    tm = m_ref.shape[0]

    @pl.when(pl.program_id(0) == 0)
    def _():
        carry_ref[...] = jnp.zeros_like(carry_ref)

    x = x_ref[...] + g1_ref[0] * jnp.dot(m_ref[...], w_ref[0], preferred_element_type=F32)
    xo_ref[...] = x
    y = x * lax.rsqrt(jnp.mean(x * x, axis=-1, keepdims=True) + EPS) * ng_ref[...]
    h = y * (1.0 + sc_ref[0]) + sh_ref[0]
    h_ref[...] = h

    h_hi = h.astype(BF16)
    h_lo = (h - h_hi.astype(F32)).astype(BF16)
    hh = jnp.dot(h_hi, wr_ref[...], preferred_element_type=F32)
    logits = hh[:, 0:LANES] + (hh[:, LANES:] + jnp.dot(h_lo, wr_ref[:, 0:LANES], preferred_element_type=F32))
    lt = logits.T
    sc = [jax.nn.sigmoid(lt[r:r + 1, :]) for r in range(N_EXPERTS)]
    bi = [sc[r] + rb_ref[r:r + 1, :] for r in range(N_EXPERTS)]
    gs = []
    for g in range(N_GROUPS):
        b0, b1, b2, b3 = (bi[j * N_GROUPS + g] for j in range(PER_GROUP))
        hi01, lo01 = jnp.maximum(b0, b1), jnp.minimum(b0, b1)
        hi23, lo23 = jnp.maximum(b2, b3), jnp.minimum(b2, b3)
        gs.append(jnp.maximum(hi01, hi23) + jnp.maximum(jnp.minimum(hi01, hi23), jnp.maximum(lo01, lo23)))
    gmax = jnp.maximum(jnp.maximum(gs[0], gs[1]), jnp.maximum(gs[2], gs[3]))
    best = jnp.where(gs[0] == gmax, 0, jnp.where(gs[1] == gmax, 1, jnp.where(gs[2] == gmax, 2, 3)))

    def pick(vals, idx):
        return jnp.where(idx == 0, vals[0], jnp.where(idx == 1, vals[1], jnp.where(idx == 2, vals[2], vals[3])))

    v = [pick([bi[j * N_GROUPS + g] for g in range(N_GROUPS)], best) for j in range(PER_GROUP)]
    s = [pick([sc[j * N_GROUPS + g] for g in range(N_GROUPS)], best) for j in range(PER_GROUP)]

    def first_max(vals):
        mx = jnp.maximum(jnp.maximum(vals[0], vals[1]), jnp.maximum(vals[2], vals[3]))
        return jnp.where(vals[0] == mx, 0, jnp.where(vals[1] == mx, 1, jnp.where(vals[2] == mx, 2, 3)))

    i1 = first_max(v)
    i2 = first_max([jnp.where(i1 == j, -jnp.inf, v[j]) for j in range(PER_GROUP)])
    s1, s2 = pick(s, i1), pick(s, i2)
    tot = s1 + s2
    e1 = best * PER_GROUP + i1
    e2 = best * PER_GROUP + i2
    e_ref[0:1, :] = e1
    e_ref[1:2, :] = e2

    eio = lax.broadcasted_iota(I32, (N_EXPERTS, tm), 0)
    hit1, hit2 = eio == e1, eio == e2
    cnt = (hit1 | hit2).astype(BF16)
    r = lax.broadcasted_iota(I32, (tm, tm), 0)
    c = lax.broadcasted_iota(I32, (tm, tm), 1)
    before = (r < c).astype(BF16)
    run = jnp.dot(cnt, before, preferred_element_type=F32) + carry_ref[:, 0:1]
    rk_ref[0:1, :] = jnp.sum(jnp.where(hit1, run, 0.0), axis=0, keepdims=True).astype(I32)
    rk_ref[1:2, :] = jnp.sum(jnp.where(hit2, run, 0.0), axis=0, keepdims=True).astype(I32)
    carry_ref[...] = carry_ref[...] + jnp.sum(cnt.astype(F32), axis=1, keepdims=True)
    cnt_ref[...] = carry_ref[...]

    rid = lax.broadcasted_iota(I32, (LANES, tm), 0)
    wrow = jnp.where(rid == 0, s1 / tot, jnp.where(rid == 1, s2 / tot, 0.0))
    wt_ref[...] = wrow.T[:, 0:8]


def _outproj(merged, w_out, l, x2, mod, norm_g, wr, rb, T):
    N, D = x2.shape
    tm = min(512, T)
    tpb = T // tm
    mspec = lambda k: pl.BlockSpec((1, 1, D), lambda i: ((i // tpb) * N_MOD + k, 0, 0))
    return pl.pallas_call(
        _outproj_kernel,
        grid=(N // tm,),
        in_specs=[pl.BlockSpec((tm, D), lambda i: (i, 0)),
                  pl.BlockSpec((1, D, D), lambda i: (l, 0, 0), pipeline_mode=pl.Buffered(1)),
                  pl.BlockSpec((tm, D), lambda i: (i, 0)),
                  mspec(2),
                  pl.BlockSpec((1, D), lambda i: (0, 0)),
                  mspec(4), mspec(3),
                  pl.BlockSpec((D, 2 * LANES), lambda i: (0, 0)),
                  pl.BlockSpec((N_EXPERTS, 1), lambda i: (0, 0))],
        out_specs=[pl.BlockSpec((tm, D), lambda i: (i, 0)),
                   pl.BlockSpec((tm, D), lambda i: (i, 0)),
                   pl.BlockSpec((2, tm), lambda i: (0, i)),
                   pl.BlockSpec((2, tm), lambda i: (0, i)),
                   pl.BlockSpec((tm, 8), lambda i: (i, 0)),
                   pl.BlockSpec((N_EXPERTS, LANES), lambda i: (0, 0))],
        out_shape=[jax.ShapeDtypeStruct((N, D), F32),
                   jax.ShapeDtypeStruct((N, D), F32),
                   jax.ShapeDtypeStruct((2, N), I32),
                   jax.ShapeDtypeStruct((2, N), I32),
                   jax.ShapeDtypeStruct((N, 8), F32),
                   jax.ShapeDtypeStruct((N_EXPERTS, LANES), F32)],
        scratch_shapes=[pltpu.VMEM((N_EXPERTS, LANES), F32)],
        compiler_params=_params(("arbitrary",)),
        name="outproj_router",
    )(merged, w_out, x2, mod, norm_g.reshape(1, D), mod, mod, wr, rb)


DMA_UNROLL = 8


def _dispatch_kernel(zb_ref, dest_ref, h_ref, xs_ref, zero_ref, sem, zsem):
    tm = h_ref.shape[0]

    @pl.when(pl.program_id(0) == 0)
    def _():
        zero_ref[...] = jnp.zeros_like(zero_ref)

        def zero_copy(n):
            start = pl.multiple_of(zb_ref[n] * MOE_ROWS, MOE_ROWS)
            return pltpu.make_async_copy(zero_ref, xs_ref.at[pl.ds(start, MOE_ROWS), :], zsem)

        for n in range(2 * N_EXPERTS):
            @pl.when(zb_ref[n] >= 0)
            def _():
                zero_copy(n).start()
        for n in range(2 * N_EXPERTS):
            @pl.when(zb_ref[n] >= 0)
            def _():
                zero_copy(n).wait()

    def row_copy(t, k):
        return pltpu.make_async_copy(h_ref.at[pl.ds(t, 1), :], xs_ref.at[pl.ds(dest_ref[k, t], 1), :], sem)

    def issue(t, carry):
        row_copy(t, 0).start()
        row_copy(t, 1).start()
        return carry

    def drain(t, carry):
        row_copy(t, 0).wait()
        row_copy(t, 1).wait()
        return carry

    lax.fori_loop(0, tm, issue, 0, unroll=DMA_UNROLL)
    lax.fori_loop(0, tm, drain, 0, unroll=DMA_UNROLL)


def _dispatch(zero_blocks, dest, h, n_rows, T):
    N, D = h.shape
    tm = min(512, T)
    return pl.pallas_call(
        _dispatch_kernel,
        grid_spec=pltpu.PrefetchScalarGridSpec(
            num_scalar_prefetch=1,
            grid=(N // tm,),
            in_specs=[pl.BlockSpec((2, tm), lambda i, zb: (0, i), memory_space=pltpu.SMEM),
                      pl.BlockSpec((tm, D), lambda i, zb: (i, 0))],
            out_specs=pl.BlockSpec(memory_space=pl.ANY),
            scratch_shapes=[pltpu.VMEM((MOE_ROWS, D), F32), pltpu.SemaphoreType.DMA(()),
                            pltpu.SemaphoreType.DMA(())]),
        out_shape=jax.ShapeDtypeStruct((n_rows, D), F32),
        compiler_params=_params(("arbitrary",)),
        name="moe_dispatch",
    )(zero_blocks, dest, h)


def _experts_kernel(be_ref, nb_ref, x_ref, wg_ref, wu_ref, wd_ref, y_ref):
    @pl.when(pl.program_id(0) < nb_ref[0])
    def _():
        x = x_ref[...].astype(BF16)
        a = jnp.dot(x, wg_ref[0, 0], preferred_element_type=F32)
        b = jnp.dot(x, wu_ref[0, 0], preferred_element_type=F32)
        hmid = (a * jax.nn.sigmoid(a) * b).astype(BF16)
        y_ref[...] = jnp.dot(hmid, wd_ref[0, 0], preferred_element_type=F32)

    @pl.when(pl.program_id(0) >= nb_ref[0])
    def _():
        y_ref[...] = jnp.zeros_like(y_ref)


def _experts(blk_e, n_used, xs, wg, wu, wd, l):
    R, D = xs.shape
    F = wg.shape[3]
    nb = R // MOE_ROWS
    row = lambda i, be, nu: (jnp.minimum(i, nu[0] - 1), 0)
    return pl.pallas_call(
        _experts_kernel,
        grid_spec=pltpu.PrefetchScalarGridSpec(
            num_scalar_prefetch=2,
            grid=(nb,),
            in_specs=[pl.BlockSpec((MOE_ROWS, D), row),
                      pl.BlockSpec((1, 1, D, F), lambda i, be, nu: (l, be[i], 0, 0)),
                      pl.BlockSpec((1, 1, D, F), lambda i, be, nu: (l, be[i], 0, 0)),
                      pl.BlockSpec((1, 1, F, D), lambda i, be, nu: (l, be[i], 0, 0))],
            out_specs=pl.BlockSpec((MOE_ROWS, D), lambda i, be, nu: (i, 0))),
        out_shape=jax.ShapeDtypeStruct((R, D), F32),
        compiler_params=_params(("arbitrary",)),
        name="moe_experts",
    )(blk_e, n_used, xs, wg, wu, wd)


def _combine_kernel(dest_ref, y_ref, wt_ref, x_ref, g2_ref, fg_ref, o_ref, buf_ref, sem, *, final):
    tm = x_ref.shape[0]

    def row_copy(t, k):
        return pltpu.make_async_copy(y_ref.at[pl.ds(dest_ref[k, t], 1), :], buf_ref.at[k, pl.ds(t, 1), :], sem)

    def issue(t, carry):
        row_copy(t, 0).start()
        row_copy(t, 1).start()
        return carry

    def drain(t, carry):
        row_copy(t, 0).wait()
        row_copy(t, 1).wait()
        return carry

    lax.fori_loop(0, tm, issue, 0, unroll=DMA_UNROLL)
    lax.fori_loop(0, tm, drain, 0, unroll=DMA_UNROLL)
    wt = wt_ref[...]
    moe = buf_ref[0] * wt[:, 0:1] + buf_ref[1] * wt[:, 1:2]
    x = x_ref[...] + g2_ref[0] * moe
    if final:
        x = x * lax.rsqrt(jnp.mean(x * x, axis=-1, keepdims=True) + EPS) * fg_ref[...]
    o_ref[...] = x


def _combine(dest, y, wt, x2, mod, final_g, T, final):
    N, D = x2.shape
    tm = min(256, T)
    tpb = T // tm
    return pl.pallas_call(
        functools.partial(_combine_kernel, final=final),
        grid=(N // tm,),
        in_specs=[pl.BlockSpec((2, tm), lambda i: (0, i), memory_space=pltpu.SMEM),
                  pl.BlockSpec(memory_space=pl.ANY),
                  pl.BlockSpec((tm, 8), lambda i: (i, 0)),
                  pl.BlockSpec((tm, D), lambda i: (i, 0)),
                  pl.BlockSpec((1, 1, D), lambda i: ((i // tpb) * N_MOD + 5, 0, 0)),
                  pl.BlockSpec((1, D), lambda i: (0, 0))],
        out_specs=pl.BlockSpec((tm, D), lambda i: (i, 0)),
        out_shape=jax.ShapeDtypeStruct((N, D), F32),
        scratch_shapes=[pltpu.VMEM((2, tm, D), F32), pltpu.SemaphoreType.DMA(())],
        compiler_params=_params(("arbitrary",)),
        name="moe_combine",
    )(dest, y, wt, x2, mod, final_g.reshape(1, D))


def _pack_w_in(w):
    L, D, _ = w.shape
    w = w.astype(BF16)
    widths = (GM_WIDTH, GM_WIDTH, 3 * DN_WIDTH, DN_WIDTH, DN_HEADS, DN_HEADS, ML_RANK, ML_RANK, ML_ROPE,
              3 * D)
    offs = [0]
    for wd in widths:
        offs.append(offs[-1] + wd)
    u, v, qkv, z, b, a, cq, ckv, kr, gates = (w[:, :, offs[i]:offs[i + 1]] for i in range(len(widths)))
    kr_rot = jnp.concatenate([-kr[:, :, ML_ROPE // 2:], kr[:, :, :ML_ROPE // 2]], axis=2)
    zeros = jnp.zeros((L, D, PROJ_COLS - OFF_BA - 2 * DN_HEADS), BF16)
    return jnp.concatenate([u, v, qkv, z, gates, cq, ckv, kr, kr_rot, b, a, zeros], axis=2)


def _block_table(counts, n_blocks):
    per = (counts + MOE_ROWS - 1) // MOE_ROWS
    ends = jnp.cumsum(per)
    first_row = (ends - per) * MOE_ROWS
    blk_e = jnp.minimum(jnp.searchsorted(ends, jnp.arange(n_blocks, dtype=I32), side="right"),
                        N_EXPERTS - 1).astype(I32)
    n_used = ends[-1]
    tail = n_used + jnp.arange(N_EXPERTS, dtype=I32)
    zero_blocks = jnp.concatenate([jnp.where(per > 0, ends - 1, -1), jnp.where(tail < n_blocks, tail, -1)])
    return blk_e, n_used.reshape(1).astype(I32), first_row.astype(I32), zero_blocks.astype(I32)


def kernel(x, c, positions, w_ada, b_ada, norm1_g, norm2_g, w_in, gmlp_ln_g, gmlp_ln_b, gmlp_ws, gmlp_bs, dn_conv_w, dn_a_log, dn_dt_bias, dn_norm_g, mla_cq_g, mla_ckv_g, mla_w_uq, mla_w_ukv, w_br_gmlp, w_br_dn, w_br_mla, w_out, w_router, router_bias, w_gate, w_up, w_down, final_norm_g):
    B, T, D = x.shape
    L = w_in.shape[0]
    N = B * T
    x2 = x.reshape(N, D)
    pos_col = positions.reshape(N, 1)
    mod_all = _ada_mod(c, w_ada, b_ada).reshape(L, B * N_MOD, 1, D)

    perm = jnp.array([(r % N_GROUPS) * PER_GROUP + r // N_GROUPS for r in range(N_EXPERTS)], I32)
    wr = jnp.concatenate([w_router[:, perm], jnp.zeros((D, LANES - N_EXPERTS), F32)], axis=1)
    wr_hi = wr.astype(BF16)
    wr_hl = jnp.concatenate([wr_hi, (wr - wr_hi.astype(F32)).astype(BF16)], axis=1)
    rb = router_bias[perm].reshape(N_EXPERTS, 1)
    n_blocks = (2 * N) // MOE_ROWS + N_EXPERTS

    w_in_p = _pack_w_in(w_in)
    wq_p, wkv_p = _pack_mla_weights(mla_w_uq, mla_w_ukv)
    w_a, w_b, w_c, w_o = (w.astype(BF16) for w in (w_br_gmlp, w_br_dn, w_br_mla, w_out))
    w_g, w_u, w_d = (w.astype(BF16) for w in (w_gate, w_up, w_down))
    rope_cs = _rope_tables(pos_col, T)

    for l in range(L):
        mod = mod_all[l]
        proj = _inproj(x2, norm1_g[l], mod, w_in_p, l, T)
        ya = _gmlp(proj, gmlp_ln_g[l], gmlp_ln_b[l], gmlp_ws[l], gmlp_bs[l], T)
        yb = _deltanet(proj, dn_conv_w[l], dn_a_log[l], dn_dt_bias[l], dn_norm_g[l], B, T)
        q, k, v = _mla_proj(proj, rope_cs, mla_cq_g[l], mla_ckv_g[l], wq_p, wkv_p, l, B, T)
        yc = _flash(q, k, v).reshape(N, ML_HEADS * ML_V)
        merged = _merge(ya, yb, yc, proj, w_a, w_b, w_c, l, T)
        x2, h2, e, rank, wt, counts = _outproj(merged, w_o, l, x2, mod, norm2_g[l], wr_hl, rb, T)
        blk_e, n_used, first_row, zero_blocks = _block_table(counts[:, 0].astype(I32), n_blocks)
        onehot = e[:, :, None] == jnp.arange(N_EXPERTS, dtype=I32)
        dest = rank + jnp.sum(jnp.where(onehot, first_row, 0), axis=-1)
        xs = _dispatch(zero_blocks, dest, h2, n_blocks * MOE_ROWS, T)
        y = _experts(blk_e, n_used, xs, w_g, w_u, w_d, l)
        x2 = _combine(dest, y, wt, x2, mod, final_norm_g, T, final=(l == L - 1))
    return x2.reshape(B, T, D)
```

```python
import functools
import math

import jax
import jax.numpy as jnp
from jax import lax
from jax.experimental import pallas as pl
from jax.experimental.pallas import tpu as pltpu

F32 = jnp.float32
BF16 = jnp.bfloat16
I32 = jnp.int32

EPS = 1e-6
N_MOD = 6
GM_GROUPS, GM_DIM, GM_CHUNK = 8, 128, 128
GM_WIDTH = GM_GROUPS * GM_DIM
DN_HEADS, DN_DIM, DN_CONV = 8, 128, 4
DN_WIDTH = DN_HEADS * DN_DIM
DN_CHUNK = 128
DN_BASE = 16
ML_HEADS, ML_RANK, ML_NOPE, ML_ROPE, ML_V = 8, 512, 128, 64, 128
ML_QK = ML_NOPE + ML_ROPE
ROPE_THETA = 10000.0
N_EXPERTS, N_GROUPS, PER_GROUP, D_EXPERT = 16, 4, 4, 1024
MOE_ROWS = 256

LANES = 128
OFF_U, OFF_V, OFF_Q, OFF_K, OFF_VV, OFF_Z = 0, 1024, 2048, 3072, 4096, 5120
OFF_GATES = 6144
OFF_CQ = OFF_GATES + 3 * 2048
OFF_CKV = OFF_CQ + ML_RANK
OFF_KR = OFF_CKV + ML_RANK
OFF_BA = OFF_KR + LANES
PROJ_TN = 1536
PROJ_COLS = 13824

VMEM_LIMIT = 52 * 1024 * 1024


def _params(sem, vmem=VMEM_LIMIT):
    return pltpu.CompilerParams(dimension_semantics=sem, vmem_limit_bytes=vmem)


def _ada_kernel(c_ref, w_ref, b_ref, o_ref):
    c = c_ref[...]
    ca = (c * jax.nn.sigmoid(c)).astype(BF16)
    o_ref[0] = jnp.dot(ca, w_ref[0].astype(BF16), preferred_element_type=F32) + b_ref[0]


def _ada_mod(c, w_ada, b_ada):
    L, D, M = w_ada.shape
    B = c.shape[0]
    tn = 1024
    return pl.pallas_call(
        _ada_kernel,
        grid=(L, M // tn),
        in_specs=[pl.BlockSpec((B, D), lambda l, j: (0, 0)),
                  pl.BlockSpec((1, D, tn), lambda l, j: (l, 0, j)),
                  pl.BlockSpec((1, 1, tn), lambda l, j: (l, 0, j))],
        out_specs=pl.BlockSpec((1, B, tn), lambda l, j: (l, 0, j)),
        out_shape=jax.ShapeDtypeStruct((L, B, M), F32),
        compiler_params=_params(("parallel", "parallel")),
        name="ada_mod",
    )(c, w_ada, b_ada.reshape(L, 1, M))


def _inproj_kernel(x_ref, g_ref, sc_ref, sh_ref, w_ref, o_ref, h_ref):
    @pl.when(pl.program_id(1) == 0)
    def _():
        x = x_ref[...]
        y = x * lax.rsqrt(jnp.mean(x * x, axis=-1, keepdims=True) + EPS) * g_ref[...]
        h_ref[...] = (y * (1.0 + sc_ref[0]) + sh_ref[0]).astype(BF16)

    o_ref[...] = jnp.dot(h_ref[...], w_ref[0], preferred_element_type=F32).astype(BF16)


def _inproj(x2, g, mod, w, l, T):
    N, D = x2.shape
    C = w.shape[2]
    tm = min(1024, T)
    tpb = T // tm
    return pl.pallas_call(
        _inproj_kernel,
        grid=(N // tm, C // PROJ_TN),
        in_specs=[pl.BlockSpec((tm, D), lambda i, j: (i, 0)),
                  pl.BlockSpec((1, D), lambda i, j: (0, 0)),
                  pl.BlockSpec((1, 1, D), lambda i, j: ((i // tpb) * N_MOD + 1, 0, 0)),
                  pl.BlockSpec((1, 1, D), lambda i, j: ((i // tpb) * N_MOD + 0, 0, 0)),
                  pl.BlockSpec((1, D, PROJ_TN), lambda i, j: (l, 0, j))],
        out_specs=pl.BlockSpec((tm, PROJ_TN), lambda i, j: (i, j)),
        out_shape=jax.ShapeDtypeStruct((N, C), BF16),
        scratch_shapes=[pltpu.VMEM((tm, D), BF16)],
        compiler_params=_params(("parallel", "arbitrary")),
        name="inproj",
    )(x2, g.reshape(1, D), mod, mod, w)


def _gmlp_kernel(u_ref, v_ref, lg_ref, lb_ref, ws_ref, bs_ref, o_ref):
    tm = u_ref.shape[0]
    v = jax.nn.gelu(v_ref[...].astype(F32))
    mu = jnp.mean(v, axis=-1, keepdims=True)
    vc = v - mu
    var = jnp.mean(vc * vc, axis=-1, keepdims=True)
    vn = (vc * lax.rsqrt(var + EPS) * lg_ref[...] + lb_ref[...]).astype(BF16)
    row = lax.broadcasted_iota(I32, (GM_CHUNK, GM_CHUNK), 0)
    col = lax.broadcasted_iota(I32, (GM_CHUNK, GM_CHUNK), 1)
    causal = col <= row
    for g in range(GM_GROUPS):
        w = jnp.where(causal, ws_ref[g], 0.0).astype(BF16)
        lo, hi = g * GM_DIM, (g + 1) * GM_DIM
        for c in range(tm // GM_CHUNK):
            r0, r1 = c * GM_CHUNK, (c + 1) * GM_CHUNK
            s = jnp.dot(w, vn[r0:r1, lo:hi], preferred_element_type=F32) + bs_ref[g]
            u = jax.nn.gelu(u_ref[r0:r1, lo:hi].astype(F32))
            o_ref[r0:r1, lo:hi] = (u * s).astype(BF16)


def _gmlp(proj, ln_g, ln_b, ws, bs, T):
    N = proj.shape[0]
    tm = min(512, T)
    W = GM_WIDTH
    return pl.pallas_call(
        _gmlp_kernel,
        grid=(N // tm,),
        in_specs=[pl.BlockSpec((tm, W), lambda i: (i, OFF_U // W)),
                  pl.BlockSpec((tm, W), lambda i: (i, OFF_V // W)),
                  pl.BlockSpec((1, W), lambda i: (0, 0)),
                  pl.BlockSpec((1, W), lambda i: (0, 0)),
                  pl.BlockSpec((GM_GROUPS, GM_CHUNK, GM_CHUNK), lambda i: (0, 0, 0)),
                  pl.BlockSpec((GM_GROUPS, GM_CHUNK, 1), lambda i: (0, 0, 0))],
        out_specs=pl.BlockSpec((tm, W), lambda i: (i, 0)),
        out_shape=jax.ShapeDtypeStruct((N, W), BF16),
        compiler_params=_params(("parallel",)),
        name="gmlp",
    )(proj, proj, ln_g.reshape(1, W), ln_b.reshape(1, W), ws, bs.reshape(GM_GROUPS, GM_CHUNK, 1))


def _mm(a, b):
    return jnp.dot(a.astype(BF16), b.astype(BF16), preferred_element_type=F32)


def _unit_lower_inverses(lows):
    C = lows[0].shape[0]
    n = range(len(lows))
    r = lax.broadcasted_iota(I32, (C, C), 0)
    c = lax.broadcasted_iota(I32, (C, C), 1)
    eye = (r == c).astype(F32)
    s = DN_BASE
    diag = (r // s) == (c // s)
    p = [-jnp.where(diag, lows[i], 0.0) for i in n]
    t = [eye + p[i] for i in n]
    for _ in range(int(math.log2(s)) - 1):
        p = [_mm(p[i], p[i]) for i in n]
        tp = [_mm(t[i], p[i]) for i in n]
        t = [t[i] + tp[i] for i in n]
    while s < C:
        band = ((r // (2 * s)) == (c // (2 * s))) & ((r // s) != (c // s))
        ot = [_mm(jnp.where(band, lows[i], 0.0), t[i]) for i in n]
        tot = [_mm(t[i], ot[i]) for i in n]
        t = [t[i] - tot[i] for i in n]
        s *= 2
    return t


def _dn_kernel(q_ref, k_ref, v_ref, z_ref, ba_ref, cw_ref, alog_ref, dtb_ref, ng_ref, o_ref,
               ext_ref, st_ref):
    tc = q_ref.shape[0]
    C = DN_CHUNK

    @pl.when(pl.program_id(1) == 0)
    def _():
        ext_ref[:, 0:8, :] = jnp.zeros((3, 8, DN_WIDTH), F32)
        st_ref[...] = jnp.zeros_like(st_ref)

    @pl.when(pl.program_id(1) != 0)
    def _():
        ext_ref[:, 0:8, :] = ext_ref[:, tc:tc + 8, :]

    ext_ref[0, 8:, :] = q_ref[...].astype(F32)
    ext_ref[1, 8:, :] = k_ref[...].astype(F32)
    ext_ref[2, 8:, :] = v_ref[...].astype(F32)

    def conv_silu(i):
        acc = jnp.zeros((tc, DN_WIDTH), F32)
        for kk in range(DN_CONV):
            acc = acc + cw_ref[i, kk:kk + 1, :] * ext_ref[i, pl.ds(8 - (DN_CONV - 1) + kk, tc), :]
        return acc * jax.nn.sigmoid(acc)

    qa, ka, va = conv_silu(0), conv_silu(1), conv_silu(2)

    ba = ba_ref[...].astype(F32)
    beta_all = jax.nn.sigmoid(ba)
    g_all = -jnp.exp(alog_ref[...]) * jax.nn.softplus(ba + dtb_ref[...])
    pos = lax.broadcasted_iota(I32, (tc, LANES), 0) % C
    gc_all = g_all
    sh = 1
    while sh < C:
        gc_all = gc_all + jnp.where(pos >= sh, pltpu.roll(gc_all, sh, 0), 0.0)
        sh *= 2
    gc_rows = gc_all.T

    r = lax.broadcasted_iota(I32, (C, C), 0)
    c = lax.broadcasted_iota(I32, (C, C), 1)
    incl = c <= r
    strict = c < r

    H = DN_HEADS
    nc = tc // C
    idx = [(ci, h) for ci in range(nc) for h in range(H)]
    G = range(len(idx))

    def piece(x, ci, h):
        return x[ci * C:(ci + 1) * C, h * DN_DIM:(h + 1) * DN_DIM]

    def l2n(x):
        return x * lax.rsqrt(jnp.sum(x * x, axis=-1, keepdims=True) + EPS)

    q = [l2n(piece(qa, ci, h)) * (DN_DIM ** -0.5) for ci, h in idx]
    k = [l2n(piece(ka, ci, h)) for ci, h in idx]
    v = [piece(va, ci, h) for ci, h in idx]
    beta = [beta_all[ci * C:(ci + 1) * C, h:h + 1] for ci, h in idx]
    g_c = [gc_all[ci * C:(ci + 1) * C, H + h:H + h + 1] for ci, h in idx]
    g_r = [gc_rows[H + h:H + h + 1, ci * C:(ci + 1) * C] for ci, h in idx]
    g_last = [g_r[i][:, C - 1:C] for i in G]
    decay = [jnp.exp(jnp.where(incl, g_c[i] - g_r[i], -1e30)) for i in G]
    kb = [k[i] * beta[i] for i in G]
    kt = [k[i].T for i in G]
    ktb = [kt[i].astype(BF16) for i in G]
    kk = [jnp.dot(kb[i].astype(BF16), ktb[i], preferred_element_type=F32) for i in G]
    qk = [jnp.dot(q[i].astype(BF16), ktb[i], preferred_element_type=F32) for i in G]
    tinv = _unit_lower_inverses([jnp.where(strict, kk[i] * decay[i], 0.0) for i in G])
    intra = [qk[i] * decay[i] for i in G]
    eg = [jnp.exp(g_c[i]) for i in G]
    u = [_mm(tinv[i], v[i] * beta[i]) for i in G]
    w = [_mm(tinv[i], kb[i] * eg[i]) for i in G]
    qg = [q[i] * eg[i] for i in G]
    k_dec_t = [kt[i] * jnp.exp(g_last[i] - g_r[i]) for i in G]

    state = [st_ref[h] for h in range(H)]
    for ci in range(nc):
        ids = [ci * H + h for h in range(H)]
        ws = [_mm(w[i], state[h]) for h, i in enumerate(ids)]
        qs = [_mm(qg[i], state[h]) for h, i in enumerate(ids)]
        v_new = [u[i] - ws[h] for h, i in enumerate(ids)]
        av = [_mm(intra[i], v_new[h]) for h, i in enumerate(ids)]
        kv = [_mm(k_dec_t[i], v_new[h]) for h, i in enumerate(ids)]
        state = [state[h] * jnp.exp(g_last[i]) + kv[h] for h, i in enumerate(ids)]
        for h in range(H):
            o_c = qs[h] + av[h]
            o_n = o_c * lax.rsqrt(jnp.mean(o_c * o_c, axis=-1, keepdims=True) + EPS) * ng_ref[...]
            z = piece(z_ref, ci, h).astype(F32)
            o_ref[ci * C:(ci + 1) * C, h * DN_DIM:(h + 1) * DN_DIM] = (o_n * (z * jax.nn.sigmoid(z))).astype(BF16)
    for h in range(H):
        st_ref[h] = state[h]


def _deltanet(proj, conv_w, a_log, dt_bias, norm_g, B, T):
    N = proj.shape[0]
    tc = min(256, T)
    tpb = T // tc
    W = DN_WIDTH
    cw = conv_w.reshape(DN_CONV, 3, W).transpose(1, 0, 2)
    pad = jnp.zeros((LANES - 2 * DN_HEADS,), F32)
    alog = jnp.concatenate([jnp.zeros((DN_HEADS,), F32), a_log, pad]).reshape(1, LANES)
    dtb = jnp.concatenate([jnp.zeros((DN_HEADS,), F32), dt_bias, pad]).reshape(1, LANES)
    blk = lambda off: pl.BlockSpec((tc, W), lambda b, t: (b * tpb + t, off // W))
    return pl.pallas_call(
        _dn_kernel,
        grid=(B, tpb),
        in_specs=[blk(OFF_Q), blk(OFF_K), blk(OFF_VV), blk(OFF_Z),
                  pl.BlockSpec((tc, LANES), lambda b, t: (b * tpb + t, OFF_BA // LANES)),
                  pl.BlockSpec((3, DN_CONV, W), lambda b, t: (0, 0, 0)),
                  pl.BlockSpec((1, LANES), lambda b, t: (0, 0)),
                  pl.BlockSpec((1, LANES), lambda b, t: (0, 0)),
                  pl.BlockSpec((1, DN_DIM), lambda b, t: (0, 0))],
        out_specs=pl.BlockSpec((tc, W), lambda b, t: (b * tpb + t, 0)),
        out_shape=jax.ShapeDtypeStruct((N, W), BF16),
        scratch_shapes=[pltpu.VMEM((3, tc + 8, W), F32),
                        pltpu.VMEM((DN_HEADS, DN_DIM, DN_DIM), F32)],
        compiler_params=_params(("parallel", "arbitrary")),
        name="deltanet",
    )(proj, proj, proj, proj, proj, cw, alog, dtb, norm_g.reshape(1, DN_DIM))


def _rope_kernel(pos_ref, invf_ref, o_ref):
    ang = pos_ref[...].astype(F32) * invf_ref[...]
    o_ref[0] = jnp.cos(ang)
    o_ref[1] = jnp.sin(ang)


def _rope_tables(pos_col, T):
    N = pos_col.shape[0]
    tm = min(1024, T)
    half = ROPE_THETA ** (-jnp.arange(0, ML_ROPE, 2, dtype=F32) / ML_ROPE)
    invf = jnp.tile(half, LANES // (ML_ROPE // 2)).reshape(1, LANES)
    return pl.pallas_call(
        _rope_kernel,
        grid=(N // tm,),
        in_specs=[pl.BlockSpec((tm, 1), lambda i: (i, 0)),
                  pl.BlockSpec((1, LANES), lambda i: (0, 0))],
        out_specs=pl.BlockSpec((2, tm, LANES), lambda i: (0, i, 0)),
        out_shape=jax.ShapeDtypeStruct((2, N, LANES), F32),
        compiler_params=_params(("parallel",)),
        name="rope_tables",
    )(pos_col, invf)


def _mla_proj_kernel(cq_ref, ckv_ref, kr_ref, cs_ref, gq_ref, gkv_ref, wq_ref, wkv_ref,
                     q_ref, k_ref, v_ref):
    tm = cq_ref.shape[0]
    cos, sin = cs_ref[0], cs_ref[1]
    scale = ML_QK ** -0.5 * math.log2(math.e)

    def rms(x, g):
        return (x * lax.rsqrt(jnp.mean(x * x, axis=-1, keepdims=True) + EPS) * g).astype(BF16)

    def rope(pair):
        return pair * cos + pltpu.roll(pair, ML_ROPE, 1) * sin

    cqn = rms(cq_ref[...].astype(F32), gq_ref[...])
    ckvn = rms(ckv_ref[...].astype(F32), gkv_ref[...])
    k_rope = rope(kr_ref[...].astype(F32))[:, 0:ML_ROPE].astype(BF16)
    ones = jnp.ones((tm, ML_V), BF16)
    W2 = 2 * LANES
    for h in range(ML_HEADS):
        qa = jnp.dot(cqn, wq_ref[0, :, h * W2:(h + 1) * W2], preferred_element_type=F32)
        q_ref[0, h, :, 0:ML_NOPE] = (qa[:, 0:ML_NOPE] * scale).astype(BF16)
        q_ref[0, h, :, ML_NOPE:ML_QK] = (rope(qa[:, ML_NOPE:]) * scale)[:, 0:ML_ROPE].astype(BF16)
        kva = jnp.dot(ckvn, wkv_ref[0, :, h * W2:(h + 1) * W2], preferred_element_type=F32)
        k_ref[0, h, :, 0:ML_NOPE] = kva[:, 0:ML_NOPE].astype(BF16)
        k_ref[0, h, :, ML_NOPE:ML_QK] = k_rope
        v_ref[0, h, :, 0:ML_V] = kva[:, ML_NOPE:].astype(BF16)
        v_ref[0, h, :, ML_V:] = ones


def _pack_mla_weights(w_uq, w_ukv):
    L = w_uq.shape[0]
    H, R = ML_HEADS, ML_RANK
    wq = w_uq.reshape(L, R, H, ML_QK)
    wr = wq[..., ML_NOPE:]
    wq = jnp.concatenate([wq, -wr[..., ML_ROPE // 2:], wr[..., :ML_ROPE // 2]], axis=-1)
    return wq.reshape(L, R, H * 2 * LANES).astype(BF16), w_ukv.astype(BF16)


def _mla_proj(proj, rope_cs, cq_g, ckv_g, wq, wkv, l, B, T):
    N = proj.shape[0]
    H, R = ML_HEADS, ML_RANK
    tm = min(512, T)
    tpb = T // tm
    hs = lambda d: pl.BlockSpec((1, H, tm, d), lambda i: (i // tpb, 0, i % tpb, 0))
    ws = pl.BlockSpec((1, R, H * 2 * LANES), lambda i: (l, 0, 0))
    return pl.pallas_call(
        _mla_proj_kernel,
        grid=(N // tm,),
        in_specs=[pl.BlockSpec((tm, R), lambda i: (i, OFF_CQ // R)),
                  pl.BlockSpec((tm, R), lambda i: (i, OFF_CKV // R)),
                  pl.BlockSpec((tm, LANES), lambda i: (i, OFF_KR // LANES)),
                  pl.BlockSpec((2, tm, LANES), lambda i: (0, i, 0)),
                  pl.BlockSpec((1, R), lambda i: (0, 0)),
                  pl.BlockSpec((1, R), lambda i: (0, 0)),
                  ws, ws],
        out_specs=[hs(ML_QK), hs(ML_QK), hs(2 * ML_V)],
        out_shape=[jax.ShapeDtypeStruct((B, H, T, ML_QK), BF16),
                   jax.ShapeDtypeStruct((B, H, T, ML_QK), BF16),
                   jax.ShapeDtypeStruct((B, H, T, 2 * ML_V), BF16)],
        compiler_params=_params(("parallel",)),
        name="mla_proj",
    )(proj, proj, proj, rope_cs, cq_g.reshape(1, R), ckv_g.reshape(1, R), wq, wkv)


def _flash_kernel(q_ref, k_ref, v_ref, o_ref, m_ref, l_ref, acc_ref):
    hp, tq = q_ref.shape[1], q_ref.shape[2]
    heads = range(hp)
    qi = pl.program_id(2)
    m_ref[...] = jnp.full_like(m_ref, -jnp.inf)
    l_ref[...] = jnp.zeros_like(l_ref)
    acc_ref[...] = jnp.zeros_like(acc_ref)

    def step(j, masked):
        start = pl.multiple_of(j * tq, tq)
        s = [lax.dot_general(q_ref[0, h], k_ref[0, h, pl.ds(start, tq), :], (((1,), (1,)), ((), ())),
                             preferred_element_type=F32) for h in heads]
        if masked:
            r = lax.broadcasted_iota(I32, (tq, tq), 0)
            c = lax.broadcasted_iota(I32, (tq, tq), 1)
            s = [jnp.where(c <= r, s[h], -jnp.inf) for h in heads]
        m_prev = [m_ref[h] for h in heads]
        m_new = [jnp.maximum(m_prev[h], jnp.max(s[h], axis=-1, keepdims=True)) for h in heads]
        p = [jnp.exp2(s[h] - jnp.concatenate([m_new[h]] * (tq // LANES), axis=1)) for h in heads]
        alpha = [jnp.exp2(m_prev[h] - m_new[h]) for h in heads]
        pv = [jnp.dot(p[h].astype(BF16), v_ref[0, h, pl.ds(start, tq), :], preferred_element_type=F32)
              for h in heads]
        for h in heads:
            l_ref[h] = alpha[h] * l_ref[h] + pv[h][:, ML_V:]
            acc_ref[h] = alpha[h] * acc_ref[h] + pv[h][:, :ML_V]
            m_ref[h] = m_new[h]

    def body(j, carry):
        step(j, False)
        return carry

    lax.fori_loop(0, qi, body, 0)
    step(qi, True)
    for h in heads:
        o_ref[0, :, h * ML_V:(h + 1) * ML_V] = (acc_ref[h] / l_ref[h]).astype(BF16)


FLASH_HEADS = 2


def _flash(q, k, v):
    B, H, T, _ = q.shape
    tq = min(1024, T)
    hp = FLASH_HEADS
    return pl.pallas_call(
        _flash_kernel,
        grid=(B, H // hp, T // tq),
        in_specs=[pl.BlockSpec((1, hp, tq, ML_QK), lambda b, h, i: (b, h, i, 0)),
                  pl.BlockSpec((1, hp, T, ML_QK), lambda b, h, i: (b, h, 0, 0)),
                  pl.BlockSpec((1, hp, T, 2 * ML_V), lambda b, h, i: (b, h, 0, 0))],
        out_specs=pl.BlockSpec((1, tq, hp * ML_V), lambda b, h, i: (b, i, h)),
        out_shape=jax.ShapeDtypeStruct((B, T, H * ML_V), BF16),
        scratch_shapes=[pltpu.VMEM((hp, tq, LANES), F32), pltpu.VMEM((hp, tq, LANES), F32),
                        pltpu.VMEM((hp, tq, ML_V), F32)],
        compiler_params=_params(("parallel", "parallel", "arbitrary")),
        name="mla_flash",
    )(q, k, v)


def _merge_kernel(ya_ref, yb_ref, yc_ref, ga_ref, gb_ref, gc_ref, wa_ref, wb_ref, wc_ref, o_ref):
    def br(y_ref, g_ref, w_ref):
        return jax.nn.sigmoid(g_ref[...].astype(F32)) * jnp.dot(y_ref[...], w_ref[0],
                                                                 preferred_element_type=F32)
    o_ref[...] = (br(ya_ref, ga_ref, wa_ref) + br(yb_ref, gb_ref, wb_ref)
                  + br(yc_ref, gc_ref, wc_ref)).astype(BF16)


def _merge(ya, yb, yc, proj, wa, wb, wc, l, T):
    N, W = ya.shape
    D = wa.shape[2]
    tm = min(1024, T)
    tn = 1024
    ys = pl.BlockSpec((tm, W), lambda i, j: (i, 0))
    gs = lambda k: pl.BlockSpec((tm, tn), lambda i, j: (i, (OFF_GATES + k * D) // tn + j))
    ws = pl.BlockSpec((1, W, tn), lambda i, j: (l, 0, j))
    return pl.pallas_call(
        _merge_kernel,
        grid=(N // tm, D // tn),
        in_specs=[ys, ys, ys, gs(0), gs(1), gs(2), ws, ws, ws],
        out_specs=pl.BlockSpec((tm, tn), lambda i, j: (i, j)),
        out_shape=jax.ShapeDtypeStruct((N, D), BF16),
        compiler_params=_params(("parallel", "arbitrary")),
        name="merge",
    )(ya, yb, yc, proj, proj, proj, wa, wb, wc)


def _outproj_kernel(m_ref, w_ref, x_ref, g1_ref, ng_ref, sc_ref, sh_ref, wr_ref, rb_ref,
                    xo_ref, h_ref, e_ref, rk_ref, wt_ref, cnt_ref, carry_ref, before_ref):
    tm = m_ref.shape[0]

    @pl.when(pl.program_id(0) == 0)
    def _():
        carry_ref[...] = jnp.zeros_like(carry_ref)
        r = lax.broadcasted_iota(I32, (tm, tm), 0)
        c = lax.broadcasted_iota(I32, (tm, tm), 1)
        before_ref[...] = (r < c).astype(BF16)

    x = x_ref[...] + g1_ref[0] * jnp.dot(m_ref[...], w_ref[0], preferred_element_type=F32)
    xo_ref[...] = x
    y = x * lax.rsqrt(jnp.mean(x * x, axis=-1, keepdims=True) + EPS) * ng_ref[...]
    h = y * (1.0 + sc_ref[0]) + sh_ref[0]
    h_ref[...] = h

    h_hi = h.astype(BF16)
    h_lo = (h - h_hi.astype(F32)).astype(BF16)
    hh = jnp.dot(h_hi, wr_ref[...], preferred_element_type=F32)
    logits = hh[:, 0:LANES] + (hh[:, LANES:] + jnp.dot(h_lo, wr_ref[:, 0:LANES], preferred_element_type=F32))
    lt = logits.T
    sc = [jax.nn.sigmoid(lt[r:r + 1, :]) for r in range(N_EXPERTS)]
    bi = [sc[r] + rb_ref[r:r + 1, :] for r in range(N_EXPERTS)]
    gs = []
    for g in range(N_GROUPS):
        b0, b1, b2, b3 = (bi[j * N_GROUPS + g] for j in range(PER_GROUP))
        hi01, lo01 = jnp.maximum(b0, b1), jnp.minimum(b0, b1)
        hi23, lo23 = jnp.maximum(b2, b3), jnp.minimum(b2, b3)
        gs.append(jnp.maximum(hi01, hi23) + jnp.maximum(jnp.minimum(hi01, hi23), jnp.maximum(lo01, lo23)))
    gmax = jnp.maximum(jnp.maximum(gs[0], gs[1]), jnp.maximum(gs[2], gs[3]))
    best = jnp.where(gs[0] == gmax, 0, jnp.where(gs[1] == gmax, 1, jnp.where(gs[2] == gmax, 2, 3)))

    def pick(vals, idx):
        return jnp.where(idx == 0, vals[0], jnp.where(idx == 1, vals[1], jnp.where(idx == 2, vals[2], vals[3])))

    v = [pick([bi[j * N_GROUPS + g] for g in range(N_GROUPS)], best) for j in range(PER_GROUP)]
    s = [pick([sc[j * N_GROUPS + g] for g in range(N_GROUPS)], best) for j in range(PER_GROUP)]

    def first_max(vals):
        mx = jnp.maximum(jnp.maximum(vals[0], vals[1]), jnp.maximum(vals[2], vals[3]))
        return jnp.where(vals[0] == mx, 0, jnp.where(vals[1] == mx, 1, jnp.where(vals[2] == mx, 2, 3)))

    i1 = first_max(v)
    i2 = first_max([jnp.where(i1 == j, -jnp.inf, v[j]) for j in range(PER_GROUP)])
    s1, s2 = pick(s, i1), pick(s, i2)
    tot = s1 + s2
    e1 = best * PER_GROUP + i1
    e2 = best * PER_GROUP + i2
    e_ref[0:1, :] = e1
    e_ref[1:2, :] = e2

    eio = lax.broadcasted_iota(I32, (N_EXPERTS, tm), 0)
    hit1, hit2 = eio == e1, eio == e2
    cnt = (hit1 | hit2).astype(BF16)
    run = jnp.dot(cnt, before_ref[...], preferred_element_type=F32) + carry_ref[:, 0:1]
    rk_ref[0:1, :] = jnp.sum(jnp.where(hit1, run, 0.0), axis=0, keepdims=True).astype(I32)
    rk_ref[1:2, :] = jnp.sum(jnp.where(hit2, run, 0.0), axis=0, keepdims=True).astype(I32)
    carry_ref[...] = carry_ref[...] + jnp.sum(cnt.astype(F32), axis=1, keepdims=True)
    cnt_ref[...] = carry_ref[...]

    rid = lax.broadcasted_iota(I32, (LANES, tm), 0)
    wrow = jnp.where(rid == 0, s1 / tot, jnp.where(rid == 1, s2 / tot, 0.0))
    wt_ref[...] = wrow.T[:, 0:8]


def _outproj(merged, w_out, l, x2, mod, norm_g, wr, rb, T):
    N, D = x2.shape
    tm = min(512, T)
    tpb = T // tm
    mspec = lambda k: pl.BlockSpec((1, 1, D), lambda i: ((i // tpb) * N_MOD + k, 0, 0))
    return pl.pallas_call(
        _outproj_kernel,
        grid=(N // tm,),
        in_specs=[pl.BlockSpec((tm, D), lambda i: (i, 0)),
                  pl.BlockSpec((1, D, D), lambda i: (l, 0, 0), pipeline_mode=pl.Buffered(1)),
                  pl.BlockSpec((tm, D), lambda i: (i, 0)),
                  mspec(2),
                  pl.BlockSpec((1, D), lambda i: (0, 0)),
                  mspec(4), mspec(3),
                  pl.BlockSpec((D, 2 * LANES), lambda i: (0, 0)),
                  pl.BlockSpec((N_EXPERTS, 1), lambda i: (0, 0))],
        out_specs=[pl.BlockSpec((tm, D), lambda i: (i, 0)),
                   pl.BlockSpec((tm, D), lambda i: (i, 0)),
                   pl.BlockSpec((2, tm), lambda i: (0, i)),
                   pl.BlockSpec((2, tm), lambda i: (0, i)),
                   pl.BlockSpec((tm, 8), lambda i: (i, 0)),
                   pl.BlockSpec((N_EXPERTS, LANES), lambda i: (0, 0))],
        out_shape=[jax.ShapeDtypeStruct((N, D), F32),
                   jax.ShapeDtypeStruct((N, D), F32),
                   jax.ShapeDtypeStruct((2, N), I32),
                   jax.ShapeDtypeStruct((2, N), I32),
                   jax.ShapeDtypeStruct((N, 8), F32),
                   jax.ShapeDtypeStruct((N_EXPERTS, LANES), F32)],
        scratch_shapes=[pltpu.VMEM((N_EXPERTS, LANES), F32), pltpu.VMEM((tm, tm), BF16)],
        compiler_params=_params(("arbitrary",)),
        name="outproj_router",
    )(merged, w_out, x2, mod, norm_g.reshape(1, D), mod, mod, wr, rb)


DMA_UNROLL = 8


def _dispatch_kernel(zb_ref, dest_ref, h_ref, xs_ref, zero_ref, sem, zsem):
    tm = h_ref.shape[0]

    @pl.when(pl.program_id(0) == 0)
    def _():
        zero_ref[...] = jnp.zeros_like(zero_ref)

        def zero_copy(n):
            start = pl.multiple_of(zb_ref[n] * MOE_ROWS, MOE_ROWS)
            return pltpu.make_async_copy(zero_ref, xs_ref.at[pl.ds(start, MOE_ROWS), :], zsem)

        for n in range(2 * N_EXPERTS):
            @pl.when(zb_ref[n] >= 0)
            def _():
                zero_copy(n).start()
        for n in range(2 * N_EXPERTS):
            @pl.when(zb_ref[n] >= 0)
            def _():
                zero_copy(n).wait()

    def row_copy(t, k):
        return pltpu.make_async_copy(h_ref.at[pl.ds(t, 1), :], xs_ref.at[pl.ds(dest_ref[k, t], 1), :], sem)

    def issue(t, carry):
        row_copy(t, 0).start()
        row_copy(t, 1).start()
        return carry

    def drain(t, carry):
        row_copy(t, 0).wait()
        row_copy(t, 1).wait()
        return carry

    lax.fori_loop(0, tm, issue, 0, unroll=DMA_UNROLL)
    lax.fori_loop(0, tm, drain, 0, unroll=DMA_UNROLL)


def _dispatch(zero_blocks, dest, h, n_rows, T):
    N, D = h.shape
    tm = min(512, T)
    return pl.pallas_call(
        _dispatch_kernel,
        grid_spec=pltpu.PrefetchScalarGridSpec(
            num_scalar_prefetch=1,
            grid=(N // tm,),
            in_specs=[pl.BlockSpec((2, tm), lambda i, zb: (0, i), memory_space=pltpu.SMEM),
                      pl.BlockSpec((tm, D), lambda i, zb: (i, 0))],
            out_specs=pl.BlockSpec(memory_space=pl.ANY),
            scratch_shapes=[pltpu.VMEM((MOE_ROWS, D), F32), pltpu.SemaphoreType.DMA(()),
                            pltpu.SemaphoreType.DMA(())]),
        out_shape=jax.ShapeDtypeStruct((n_rows, D), F32),
        compiler_params=_params(("arbitrary",)),
        name="moe_dispatch",
    )(zero_blocks, dest, h)


def _experts_kernel(be_ref, nb_ref, x_ref, wg_ref, wu_ref, wd_ref, y_ref):
    @pl.when(pl.program_id(0) < nb_ref[0])
    def _():
        x = x_ref[...].astype(BF16)
        a = jnp.dot(x, wg_ref[0, 0], preferred_element_type=F32)
        b = jnp.dot(x, wu_ref[0, 0], preferred_element_type=F32)
        hmid = (a * jax.nn.sigmoid(a) * b).astype(BF16)
        y_ref[...] = jnp.dot(hmid, wd_ref[0, 0], preferred_element_type=F32)

    @pl.when(pl.program_id(0) >= nb_ref[0])
    def _():
        y_ref[...] = jnp.zeros_like(y_ref)


def _experts(blk_e, n_used, xs, wg, wu, wd, l):
    R, D = xs.shape
    F = wg.shape[3]
    nb = R // MOE_ROWS
    row = lambda i, be, nu: (jnp.minimum(i, nu[0] - 1), 0)
    return pl.pallas_call(
        _experts_kernel,
        grid_spec=pltpu.PrefetchScalarGridSpec(
            num_scalar_prefetch=2,
            grid=(nb,),
            in_specs=[pl.BlockSpec((MOE_ROWS, D), row),
                      pl.BlockSpec((1, 1, D, F), lambda i, be, nu: (l, be[i], 0, 0)),
                      pl.BlockSpec((1, 1, D, F), lambda i, be, nu: (l, be[i], 0, 0)),
                      pl.BlockSpec((1, 1, F, D), lambda i, be, nu: (l, be[i], 0, 0))],
            out_specs=pl.BlockSpec((MOE_ROWS, D), lambda i, be, nu: (i, 0))),
        out_shape=jax.ShapeDtypeStruct((R, D), F32),
        compiler_params=_params(("arbitrary",)),
        name="moe_experts",
    )(blk_e, n_used, xs, wg, wu, wd)


def _combine_kernel(dest_ref, y_ref, wt_ref, x_ref, g2_ref, fg_ref, o_ref, buf_ref, sem, *, final):
    tm = x_ref.shape[0]

    def row_copy(t, k):
        return pltpu.make_async_copy(y_ref.at[pl.ds(dest_ref[k, t], 1), :], buf_ref.at[k, pl.ds(t, 1), :], sem)

    def issue(t, carry):
        row_copy(t, 0).start()
        row_copy(t, 1).start()
        return carry

    def drain(t, carry):
        row_copy(t, 0).wait()
        row_copy(t, 1).wait()
        return carry

    lax.fori_loop(0, tm, issue, 0, unroll=DMA_UNROLL)
    lax.fori_loop(0, tm, drain, 0, unroll=DMA_UNROLL)
    wt = wt_ref[...]
    moe = buf_ref[0] * wt[:, 0:1] + buf_ref[1] * wt[:, 1:2]
    x = x_ref[...] + g2_ref[0] * moe
    if final:
        x = x * lax.rsqrt(jnp.mean(x * x, axis=-1, keepdims=True) + EPS) * fg_ref[...]
    o_ref[...] = x


def _combine(dest, y, wt, x2, mod, final_g, T, final):
    N, D = x2.shape
    tm = min(512, T)
    tpb = T // tm
    return pl.pallas_call(
        functools.partial(_combine_kernel, final=final),
        grid=(N // tm,),
        in_specs=[pl.BlockSpec((2, tm), lambda i: (0, i), memory_space=pltpu.SMEM),
                  pl.BlockSpec(memory_space=pl.ANY),
                  pl.BlockSpec((tm, 8), lambda i: (i, 0)),
                  pl.BlockSpec((tm, D), lambda i: (i, 0)),
                  pl.BlockSpec((1, 1, D), lambda i: ((i // tpb) * N_MOD + 5, 0, 0)),
                  pl.BlockSpec((1, D), lambda i: (0, 0))],
        out_specs=pl.BlockSpec((tm, D), lambda i: (i, 0)),
        out_shape=jax.ShapeDtypeStruct((N, D), F32),
        scratch_shapes=[pltpu.VMEM((2, tm, D), F32), pltpu.SemaphoreType.DMA(())],
        compiler_params=_params(("arbitrary",)),
        name="moe_combine",
    )(dest, y, wt, x2, mod, final_g.reshape(1, D))


def _pack_w_in(w):
    L, D, _ = w.shape
    w = w.astype(BF16)
    widths = (GM_WIDTH, GM_WIDTH, 3 * DN_WIDTH, DN_WIDTH, DN_HEADS, DN_HEADS, ML_RANK, ML_RANK, ML_ROPE,
              3 * D)
    offs = [0]
    for wd in widths:
        offs.append(offs[-1] + wd)
    u, v, qkv, z, b, a, cq, ckv, kr, gates = (w[:, :, offs[i]:offs[i + 1]] for i in range(len(widths)))
    kr_rot = jnp.concatenate([-kr[:, :, ML_ROPE // 2:], kr[:, :, :ML_ROPE // 2]], axis=2)
    zeros = jnp.zeros((L, D, PROJ_COLS - OFF_BA - 2 * DN_HEADS), BF16)
    return jnp.concatenate([u, v, qkv, z, gates, cq, ckv, kr, kr_rot, b, a, zeros], axis=2)


def _block_table(counts, n_blocks):
    per = (counts + MOE_ROWS - 1) // MOE_ROWS
    ends = jnp.cumsum(per)
    first_row = (ends - per) * MOE_ROWS
    ends_le = ends[None, :] <= jnp.arange(n_blocks, dtype=I32)[:, None]
    blk_e = jnp.minimum(jnp.sum(ends_le.astype(I32), axis=1), N_EXPERTS - 1)
    n_used = ends[-1]
    tail = n_used + jnp.arange(N_EXPERTS, dtype=I32)
    zero_blocks = jnp.concatenate([jnp.where(per > 0, ends - 1, -1), jnp.where(tail < n_blocks, tail, -1)])
    return blk_e, n_used.reshape(1).astype(I32), first_row.astype(I32), zero_blocks.astype(I32)


def kernel(x, c, positions, w_ada, b_ada, norm1_g, norm2_g, w_in, gmlp_ln_g, gmlp_ln_b, gmlp_ws, gmlp_bs, dn_conv_w, dn_a_log, dn_dt_bias, dn_norm_g, mla_cq_g, mla_ckv_g, mla_w_uq, mla_w_ukv, w_br_gmlp, w_br_dn, w_br_mla, w_out, w_router, router_bias, w_gate, w_up, w_down, final_norm_g):
    B, T, D = x.shape
    L = w_in.shape[0]
    N = B * T
    x2 = x.reshape(N, D)
    pos_col = positions.reshape(N, 1)
    mod_all = _ada_mod(c, w_ada, b_ada).reshape(L, B * N_MOD, 1, D)

    perm = jnp.array([(r % N_GROUPS) * PER_GROUP + r // N_GROUPS for r in range(N_EXPERTS)], I32)
    wr = jnp.concatenate([w_router[:, perm], jnp.zeros((D, LANES - N_EXPERTS), F32)], axis=1)
    wr_hi = wr.astype(BF16)
    wr_hl = jnp.concatenate([wr_hi, (wr - wr_hi.astype(F32)).astype(BF16)], axis=1)
    rb = router_bias[perm].reshape(N_EXPERTS, 1)
    n_blocks = (2 * N) // MOE_ROWS + N_EXPERTS

    w_in_p = _pack_w_in(w_in)
    wq_p, wkv_p = _pack_mla_weights(mla_w_uq, mla_w_ukv)
    w_a, w_b, w_c, w_o = (w.astype(BF16) for w in (w_br_gmlp, w_br_dn, w_br_mla, w_out))
    w_g, w_u, w_d = (w.astype(BF16) for w in (w_gate, w_up, w_down))
    rope_cs = _rope_tables(pos_col, T)

    for l in range(L):
        mod = mod_all[l]
        proj = _inproj(x2, norm1_g[l], mod, w_in_p, l, T)
        ya = _gmlp(proj, gmlp_ln_g[l], gmlp_ln_b[l], gmlp_ws[l], gmlp_bs[l], T)
        yb = _deltanet(proj, dn_conv_w[l], dn_a_log[l], dn_dt_bias[l], dn_norm_g[l], B, T)
        q, k, v = _mla_proj(proj, rope_cs, mla_cq_g[l], mla_ckv_g[l], wq_p, wkv_p, l, B, T)
        yc = _flash(q, k, v).reshape(N, ML_HEADS * ML_V)
        merged = _merge(ya, yb, yc, proj, w_a, w_b, w_c, l, T)
        x2, h2, e, rank, wt, counts = _outproj(merged, w_o, l, x2, mod, norm2_g[l], wr_hl, rb, T)
        blk_e, n_used, first_row, zero_blocks = _block_table(counts[:, 0].astype(I32), n_blocks)
        onehot = e[:, :, None] == jnp.arange(N_EXPERTS, dtype=I32)
        dest = rank + jnp.sum(jnp.where(onehot, first_row, 0), axis=-1)
        xs = _dispatch(zero_blocks, dest, h2, n_blocks * MOE_ROWS, T)
        y = _experts(blk_e, n_used, xs, w_g, w_u, w_d, l)
        x2 = _combine(dest, y, wt, x2, mod, final_norm_g, T, final=(l == L - 1))
    return x2.reshape(B, T, D)
```

```python
import functools
import math

import jax
import jax.numpy as jnp
from jax import lax
from jax.experimental import pallas as pl
from jax.experimental.pallas import tpu as pltpu

F32 = jnp.float32
BF16 = jnp.bfloat16
I32 = jnp.int32

EPS = 1e-6
N_MOD = 6
GM_GROUPS, GM_DIM, GM_CHUNK = 8, 128, 128
GM_WIDTH = GM_GROUPS * GM_DIM
DN_HEADS, DN_DIM, DN_CONV = 8, 128, 4
DN_WIDTH = DN_HEADS * DN_DIM
DN_CHUNK = 128
DN_BASE = 16
ML_HEADS, ML_RANK, ML_NOPE, ML_ROPE, ML_V = 8, 512, 128, 64, 128
ML_QK = ML_NOPE + ML_ROPE
ROPE_THETA = 10000.0
N_EXPERTS, N_GROUPS, PER_GROUP, D_EXPERT = 16, 4, 4, 1024
MOE_ROWS = 256

LANES = 128
OFF_U, OFF_V, OFF_Q, OFF_K, OFF_VV, OFF_Z = 0, 1024, 2048, 3072, 4096, 5120
OFF_GATES = 6144
OFF_CQ = OFF_GATES + 3 * 2048
OFF_CKV = OFF_CQ + ML_RANK
OFF_KR = OFF_CKV + ML_RANK
OFF_BA = OFF_KR + LANES
PROJ_TN = 1536
PROJ_COLS = 13824

VMEM_LIMIT = 52 * 1024 * 1024


def _params(sem, vmem=VMEM_LIMIT):
    return pltpu.CompilerParams(dimension_semantics=sem, vmem_limit_bytes=vmem)


def _ada_kernel(c_ref, w_ref, b_ref, o_ref):
    c = c_ref[...]
    ca = (c * jax.nn.sigmoid(c)).astype(BF16)
    o_ref[0] = jnp.dot(ca, w_ref[0].astype(BF16), preferred_element_type=F32) + b_ref[0]


def _ada_mod(c, w_ada, b_ada):
    L, D, M = w_ada.shape
    B = c.shape[0]
    tn = 1024
    return pl.pallas_call(
        _ada_kernel,
        grid=(L, M // tn),
        in_specs=[pl.BlockSpec((B, D), lambda l, j: (0, 0)),
                  pl.BlockSpec((1, D, tn), lambda l, j: (l, 0, j)),
                  pl.BlockSpec((1, 1, tn), lambda l, j: (l, 0, j))],
        out_specs=pl.BlockSpec((1, B, tn), lambda l, j: (l, 0, j)),
        out_shape=jax.ShapeDtypeStruct((L, B, M), F32),
        compiler_params=_params(("parallel", "parallel")),
        name="ada_mod",
    )(c, w_ada, b_ada.reshape(L, 1, M))


def _inproj_kernel(x_ref, g_ref, sc_ref, sh_ref, w_ref, o_ref, h_ref):
    @pl.when(pl.program_id(1) == 0)
    def _():
        x = x_ref[...]
        y = x * lax.rsqrt(jnp.mean(x * x, axis=-1, keepdims=True) + EPS) * g_ref[...]
        h_ref[...] = (y * (1.0 + sc_ref[0]) + sh_ref[0]).astype(BF16)

    o_ref[...] = jnp.dot(h_ref[...], w_ref[0], preferred_element_type=F32).astype(BF16)


def _inproj(x2, g, mod, w, l, T):
    N, D = x2.shape
    C = w.shape[2]
    tm = min(1024, T)
    tpb = T // tm
    return pl.pallas_call(
        _inproj_kernel,
        grid=(N // tm, C // PROJ_TN),
        in_specs=[pl.BlockSpec((tm, D), lambda i, j: (i, 0)),
                  pl.BlockSpec((1, D), lambda i, j: (0, 0)),
                  pl.BlockSpec((1, 1, D), lambda i, j: ((i // tpb) * N_MOD + 1, 0, 0)),
                  pl.BlockSpec((1, 1, D), lambda i, j: ((i // tpb) * N_MOD + 0, 0, 0)),
                  pl.BlockSpec((1, D, PROJ_TN), lambda i, j: (l, 0, j))],
        out_specs=pl.BlockSpec((tm, PROJ_TN), lambda i, j: (i, j)),
        out_shape=jax.ShapeDtypeStruct((N, C), BF16),
        scratch_shapes=[pltpu.VMEM((tm, D), BF16)],
        compiler_params=_params(("parallel", "arbitrary")),
        name="inproj",
    )(x2, g.reshape(1, D), mod, mod, w)


def _gmlp_kernel(u_ref, v_ref, lg_ref, lb_ref, ws_ref, bs_ref, o_ref):
    tm = u_ref.shape[0]
    v = jax.nn.gelu(v_ref[...].astype(F32))
    mu = jnp.mean(v, axis=-1, keepdims=True)
    vc = v - mu
    var = jnp.mean(vc * vc, axis=-1, keepdims=True)
    vn = (vc * lax.rsqrt(var + EPS) * lg_ref[...] + lb_ref[...]).astype(BF16)
    row = lax.broadcasted_iota(I32, (GM_CHUNK, GM_CHUNK), 0)
    col = lax.broadcasted_iota(I32, (GM_CHUNK, GM_CHUNK), 1)
    causal = col <= row
    for g in range(GM_GROUPS):
        w = jnp.where(causal, ws_ref[g], 0.0).astype(BF16)
        lo, hi = g * GM_DIM, (g + 1) * GM_DIM
        for c in range(tm // GM_CHUNK):
            r0, r1 = c * GM_CHUNK, (c + 1) * GM_CHUNK
            s = jnp.dot(w, vn[r0:r1, lo:hi], preferred_element_type=F32) + bs_ref[g]
            u = jax.nn.gelu(u_ref[r0:r1, lo:hi].astype(F32))
            o_ref[r0:r1, lo:hi] = (u * s).astype(BF16)


def _gmlp(proj, ln_g, ln_b, ws, bs, T):
    N = proj.shape[0]
    tm = min(512, T)
    W = GM_WIDTH
    return pl.pallas_call(
        _gmlp_kernel,
        grid=(N // tm,),
        in_specs=[pl.BlockSpec((tm, W), lambda i: (i, OFF_U // W)),
                  pl.BlockSpec((tm, W), lambda i: (i, OFF_V // W)),
                  pl.BlockSpec((1, W), lambda i: (0, 0)),
                  pl.BlockSpec((1, W), lambda i: (0, 0)),
                  pl.BlockSpec((GM_GROUPS, GM_CHUNK, GM_CHUNK), lambda i: (0, 0, 0)),
                  pl.BlockSpec((GM_GROUPS, GM_CHUNK, 1), lambda i: (0, 0, 0))],
        out_specs=pl.BlockSpec((tm, W), lambda i: (i, 0)),
        out_shape=jax.ShapeDtypeStruct((N, W), BF16),
        compiler_params=_params(("parallel",)),
        name="gmlp",
    )(proj, proj, ln_g.reshape(1, W), ln_b.reshape(1, W), ws, bs.reshape(GM_GROUPS, GM_CHUNK, 1))


def _mm(a, b):
    return jnp.dot(a.astype(BF16), b.astype(BF16), preferred_element_type=F32)


def _unit_lower_inverses(lows):
    C = lows[0].shape[0]
    n = range(len(lows))
    r = lax.broadcasted_iota(I32, (C, C), 0)
    c = lax.broadcasted_iota(I32, (C, C), 1)
    eye = (r == c).astype(F32)
    s = DN_BASE
    diag = (r // s) == (c // s)
    p = [-jnp.where(diag, lows[i], 0.0) for i in n]
    t = [eye + p[i] for i in n]
    for _ in range(int(math.log2(s)) - 1):
        p = [_mm(p[i], p[i]) for i in n]
        tp = [_mm(t[i], p[i]) for i in n]
        t = [t[i] + tp[i] for i in n]
    while s < C:
        band = ((r // (2 * s)) == (c // (2 * s))) & ((r // s) != (c // s))
        ot = [_mm(jnp.where(band, lows[i], 0.0), t[i]) for i in n]
        tot = [_mm(t[i], ot[i]) for i in n]
        t = [t[i] - tot[i] for i in n]
        s *= 2
    return t


def _dn_kernel(q_ref, k_ref, v_ref, z_ref, ba_ref, cw_ref, alog_ref, dtb_ref, ng_ref, o_ref,
               ext_ref, st_ref):
    tc = q_ref.shape[0]
    C = DN_CHUNK

    @pl.when(pl.program_id(1) == 0)
    def _():
        ext_ref[:, 0:8, :] = jnp.zeros((3, 8, DN_WIDTH), F32)
        st_ref[...] = jnp.zeros_like(st_ref)

    @pl.when(pl.program_id(1) != 0)
    def _():
        ext_ref[:, 0:8, :] = ext_ref[:, tc:tc + 8, :]

    ext_ref[0, 8:, :] = q_ref[...].astype(F32)
    ext_ref[1, 8:, :] = k_ref[...].astype(F32)
    ext_ref[2, 8:, :] = v_ref[...].astype(F32)

    def conv_silu(i):
        acc = jnp.zeros((tc, DN_WIDTH), F32)
        for kk in range(DN_CONV):
            acc = acc + cw_ref[i, kk:kk + 1, :] * ext_ref[i, pl.ds(8 - (DN_CONV - 1) + kk, tc), :]
        return acc * jax.nn.sigmoid(acc)

    qa, ka, va = conv_silu(0), conv_silu(1), conv_silu(2)

    ba = ba_ref[...].astype(F32)
    beta_all = jax.nn.sigmoid(ba)
    g_all = -jnp.exp(alog_ref[...]) * jax.nn.softplus(ba + dtb_ref[...])
    pos = lax.broadcasted_iota(I32, (tc, LANES), 0) % C
    gc_all = g_all
    sh = 1
    while sh < C:
        gc_all = gc_all + jnp.where(pos >= sh, pltpu.roll(gc_all, sh, 0), 0.0)
        sh *= 2
    gc_rows = gc_all.T

    r = lax.broadcasted_iota(I32, (C, C), 0)
    c = lax.broadcasted_iota(I32, (C, C), 1)
    incl = c <= r
    strict = c < r

    H = DN_HEADS
    nc = tc // C
    idx = [(ci, h) for ci in range(nc) for h in range(H)]
    G = range(len(idx))

    def piece(x, ci, h):
        return x[ci * C:(ci + 1) * C, h * DN_DIM:(h + 1) * DN_DIM]

    def l2n(x):
        return x * lax.rsqrt(jnp.sum(x * x, axis=-1, keepdims=True) + EPS)

    q = [l2n(piece(qa, ci, h)) * (DN_DIM ** -0.5) for ci, h in idx]
    k = [l2n(piece(ka, ci, h)) for ci, h in idx]
    v = [piece(va, ci, h) for ci, h in idx]
    beta = [beta_all[ci * C:(ci + 1) * C, h:h + 1] for ci, h in idx]
    g_c = [gc_all[ci * C:(ci + 1) * C, H + h:H + h + 1] for ci, h in idx]
    g_r = [gc_rows[H + h:H + h + 1, ci * C:(ci + 1) * C] for ci, h in idx]
    g_last = [g_r[i][:, C - 1:C] for i in G]
    decay = [jnp.exp(jnp.where(incl, g_c[i] - g_r[i], -1e30)) for i in G]
    kb = [k[i] * beta[i] for i in G]
    kt = [k[i].T for i in G]
    ktb = [kt[i].astype(BF16) for i in G]
    kk = [jnp.dot(kb[i].astype(BF16), ktb[i], preferred_element_type=F32) for i in G]
    qk = [jnp.dot(q[i].astype(BF16), ktb[i], preferred_element_type=F32) for i in G]
    tinv = _unit_lower_inverses([jnp.where(strict, kk[i] * decay[i], 0.0) for i in G])
    intra = [qk[i] * decay[i] for i in G]
    eg = [jnp.exp(g_c[i]) for i in G]
    u = [_mm(tinv[i], v[i] * beta[i]) for i in G]
    w = [_mm(tinv[i], kb[i] * eg[i]) for i in G]
    qg = [q[i] * eg[i] for i in G]
    k_dec_t = [kt[i] * jnp.exp(g_last[i] - g_r[i]) for i in G]

    state = [st_ref[h] for h in range(H)]
    for ci in range(nc):
        ids = [ci * H + h for h in range(H)]
        ws = [_mm(w[i], state[h]) for h, i in enumerate(ids)]
        qs = [_mm(qg[i], state[h]) for h, i in enumerate(ids)]
        v_new = [u[i] - ws[h] for h, i in enumerate(ids)]
        av = [_mm(intra[i], v_new[h]) for h, i in enumerate(ids)]
        kv = [_mm(k_dec_t[i], v_new[h]) for h, i in enumerate(ids)]
        state = [state[h] * jnp.exp(g_last[i]) + kv[h] for h, i in enumerate(ids)]
        for h in range(H):
            o_c = qs[h] + av[h]
            o_n = o_c * lax.rsqrt(jnp.mean(o_c * o_c, axis=-1, keepdims=True) + EPS) * ng_ref[...]
            z = piece(z_ref, ci, h).astype(F32)
            o_ref[ci * C:(ci + 1) * C, h * DN_DIM:(h + 1) * DN_DIM] = (o_n * (z * jax.nn.sigmoid(z))).astype(BF16)
    for h in range(H):
        st_ref[h] = state[h]


def _deltanet(proj, conv_w, a_log, dt_bias, norm_g, B, T):
    N = proj.shape[0]
    tc = min(256, T)
    tpb = T // tc
    W = DN_WIDTH
    cw = conv_w.reshape(DN_CONV, 3, W).transpose(1, 0, 2)
    pad = jnp.zeros((LANES - 2 * DN_HEADS,), F32)
    alog = jnp.concatenate([jnp.zeros((DN_HEADS,), F32), a_log, pad]).reshape(1, LANES)
    dtb = jnp.concatenate([jnp.zeros((DN_HEADS,), F32), dt_bias, pad]).reshape(1, LANES)
    blk = lambda off: pl.BlockSpec((tc, W), lambda b, t: (b * tpb + t, off // W))
    return pl.pallas_call(
        _dn_kernel,
        grid=(B, tpb),
        in_specs=[blk(OFF_Q), blk(OFF_K), blk(OFF_VV), blk(OFF_Z),
                  pl.BlockSpec((tc, LANES), lambda b, t: (b * tpb + t, OFF_BA // LANES)),
                  pl.BlockSpec((3, DN_CONV, W), lambda b, t: (0, 0, 0)),
                  pl.BlockSpec((1, LANES), lambda b, t: (0, 0)),
                  pl.BlockSpec((1, LANES), lambda b, t: (0, 0)),
                  pl.BlockSpec((1, DN_DIM), lambda b, t: (0, 0))],
        out_specs=pl.BlockSpec((tc, W), lambda b, t: (b * tpb + t, 0)),
        out_shape=jax.ShapeDtypeStruct((N, W), BF16),
        scratch_shapes=[pltpu.VMEM((3, tc + 8, W), F32),
                        pltpu.VMEM((DN_HEADS, DN_DIM, DN_DIM), F32)],
        compiler_params=_params(("parallel", "arbitrary")),
        name="deltanet",
    )(proj, proj, proj, proj, proj, cw, alog, dtb, norm_g.reshape(1, DN_DIM))


def _rope_kernel(pos_ref, invf_ref, o_ref):
    ang = pos_ref[...].astype(F32) * invf_ref[...]
    o_ref[0] = jnp.cos(ang)
    o_ref[1] = jnp.sin(ang)


def _rope_tables(pos_col, T):
    N = pos_col.shape[0]
    tm = min(1024, T)
    half = ROPE_THETA ** (-jnp.arange(0, ML_ROPE, 2, dtype=F32) / ML_ROPE)
    invf = jnp.tile(half, LANES // (ML_ROPE // 2)).reshape(1, LANES)
    return pl.pallas_call(
        _rope_kernel,
        grid=(N // tm,),
        in_specs=[pl.BlockSpec((tm, 1), lambda i: (i, 0)),
                  pl.BlockSpec((1, LANES), lambda i: (0, 0))],
        out_specs=pl.BlockSpec((2, tm, LANES), lambda i: (0, i, 0)),
        out_shape=jax.ShapeDtypeStruct((2, N, LANES), F32),
        compiler_params=_params(("parallel",)),
        name="rope_tables",
    )(pos_col, invf)


def _mla_proj_kernel(cq_ref, ckv_ref, kr_ref, cs_ref, gq_ref, gkv_ref, wq_ref, wkv_ref,
                     q_ref, k_ref, v_ref):
    tm = cq_ref.shape[0]
    cos, sin = cs_ref[0], cs_ref[1]
    scale = ML_QK ** -0.5 * math.log2(math.e)

    def rms(x, g):
        return (x * lax.rsqrt(jnp.mean(x * x, axis=-1, keepdims=True) + EPS) * g).astype(BF16)

    def rope(pair):
        return pair * cos + pltpu.roll(pair, ML_ROPE, 1) * sin

    cqn = rms(cq_ref[...].astype(F32), gq_ref[...])
    ckvn = rms(ckv_ref[...].astype(F32), gkv_ref[...])
    k_rope = rope(kr_ref[...].astype(F32))[:, 0:ML_ROPE].astype(BF16)
    ones = jnp.ones((tm, ML_V), BF16)
    W2 = 2 * LANES
    for h in range(ML_HEADS):
        qa = jnp.dot(cqn, wq_ref[0, :, h * W2:(h + 1) * W2], preferred_element_type=F32)
        q_ref[0, h, :, 0:ML_NOPE] = (qa[:, 0:ML_NOPE] * scale).astype(BF16)
        q_ref[0, h, :, ML_NOPE:ML_QK] = (rope(qa[:, ML_NOPE:]) * scale)[:, 0:ML_ROPE].astype(BF16)
        kva = jnp.dot(ckvn, wkv_ref[0, :, h * W2:(h + 1) * W2], preferred_element_type=F32)
        k_ref[0, h, :, 0:ML_NOPE] = kva[:, 0:ML_NOPE].astype(BF16)
        k_ref[0, h, :, ML_NOPE:ML_QK] = k_rope
        v_ref[0, h, :, 0:ML_V] = kva[:, ML_NOPE:].astype(BF16)
        v_ref[0, h, :, ML_V:] = ones


def _pack_mla_weights(w_uq, w_ukv):
    L = w_uq.shape[0]
    H, R = ML_HEADS, ML_RANK
    wq = w_uq.reshape(L, R, H, ML_QK)
    wr = wq[..., ML_NOPE:]
    wq = jnp.concatenate([wq, -wr[..., ML_ROPE // 2:], wr[..., :ML_ROPE // 2]], axis=-1)
    return wq.reshape(L, R, H * 2 * LANES).astype(BF16), w_ukv.astype(BF16)


def _mla_proj(proj, rope_cs, cq_g, ckv_g, wq, wkv, l, B, T):
    N = proj.shape[0]
    H, R = ML_HEADS, ML_RANK
    tm = min(512, T)
    tpb = T // tm
    hs = lambda d: pl.BlockSpec((1, H, tm, d), lambda i: (i // tpb, 0, i % tpb, 0))
    ws = pl.BlockSpec((1, R, H * 2 * LANES), lambda i: (l, 0, 0))
    return pl.pallas_call(
        _mla_proj_kernel,
        grid=(N // tm,),
        in_specs=[pl.BlockSpec((tm, R), lambda i: (i, OFF_CQ // R)),
                  pl.BlockSpec((tm, R), lambda i: (i, OFF_CKV // R)),
                  pl.BlockSpec((tm, LANES), lambda i: (i, OFF_KR // LANES)),
                  pl.BlockSpec((2, tm, LANES), lambda i: (0, i, 0)),
                  pl.BlockSpec((1, R), lambda i: (0, 0)),
                  pl.BlockSpec((1, R), lambda i: (0, 0)),
                  ws, ws],
        out_specs=[hs(ML_QK), hs(ML_QK), hs(2 * ML_V)],
        out_shape=[jax.ShapeDtypeStruct((B, H, T, ML_QK), BF16),
                   jax.ShapeDtypeStruct((B, H, T, ML_QK), BF16),
                   jax.ShapeDtypeStruct((B, H, T, 2 * ML_V), BF16)],
        compiler_params=_params(("parallel",)),
        name="mla_proj",
    )(proj, proj, proj, rope_cs, cq_g.reshape(1, R), ckv_g.reshape(1, R), wq, wkv)


def _flash_kernel(q_ref, k_ref, v_ref, o_ref, m_ref, l_ref, acc_ref):
    hp, tq = q_ref.shape[1], q_ref.shape[2]
    heads = range(hp)
    qi = pl.program_id(2)
    m_ref[...] = jnp.full_like(m_ref, -jnp.inf)
    l_ref[...] = jnp.zeros_like(l_ref)
    acc_ref[...] = jnp.zeros_like(acc_ref)

    def step(row0, nr, start, nc, causal):
        rows = pl.ds(row0, nr)
        keys = pl.ds(start, nc)
        s = [lax.dot_general(q_ref[0, h, rows, :], k_ref[0, h, keys, :], (((1,), (1,)), ((), ())),
                             preferred_element_type=F32) for h in heads]
        if causal:
            r = lax.broadcasted_iota(I32, (nr, nc), 0)
            c = lax.broadcasted_iota(I32, (nr, nc), 1)
            s = [jnp.where(c <= r, s[h], -jnp.inf) for h in heads]
        m_prev = [m_ref[h, rows, :] for h in heads]
        m_new = [jnp.maximum(m_prev[h], jnp.max(s[h], axis=-1, keepdims=True)) for h in heads]
        p = [jnp.exp2(s[h] - jnp.concatenate([m_new[h]] * (nc // LANES), axis=1)) for h in heads]
        alpha = [jnp.exp2(m_prev[h] - m_new[h]) for h in heads]
        pv = [jnp.dot(p[h].astype(BF16), v_ref[0, h, keys, :], preferred_element_type=F32) for h in heads]
        for h in heads:
            l_ref[h, rows, :] = alpha[h] * l_ref[h, rows, :] + pv[h][:, ML_V:]
            acc_ref[h, rows, :] = alpha[h] * acc_ref[h, rows, :] + pv[h][:, :ML_V]
            m_ref[h, rows, :] = m_new[h]

    def body(j, carry):
        step(0, tq, pl.multiple_of(j * tq, tq), tq, False)
        return carry

    lax.fori_loop(0, qi, body, 0)
    half = tq // 2
    diag = pl.multiple_of(qi * tq, tq)
    step(0, tq, diag, half, True)
    step(half, half, pl.multiple_of(diag + half, half), half, True)
    for h in heads:
        o_ref[0, :, h * ML_V:(h + 1) * ML_V] = (acc_ref[h] / l_ref[h]).astype(BF16)


FLASH_HEADS = 2


def _flash(q, k, v):
    B, H, T, _ = q.shape
    tq = min(1024, T)
    hp = FLASH_HEADS
    return pl.pallas_call(
        _flash_kernel,
        grid=(B, H // hp, T // tq),
        in_specs=[pl.BlockSpec((1, hp, tq, ML_QK), lambda b, h, i: (b, h, i, 0)),
                  pl.BlockSpec((1, hp, T, ML_QK), lambda b, h, i: (b, h, 0, 0)),
                  pl.BlockSpec((1, hp, T, 2 * ML_V), lambda b, h, i: (b, h, 0, 0))],
        out_specs=pl.BlockSpec((1, tq, hp * ML_V), lambda b, h, i: (b, i, h)),
        out_shape=jax.ShapeDtypeStruct((B, T, H * ML_V), BF16),
        scratch_shapes=[pltpu.VMEM((hp, tq, LANES), F32), pltpu.VMEM((hp, tq, LANES), F32),
                        pltpu.VMEM((hp, tq, ML_V), F32)],
        compiler_params=_params(("parallel", "parallel", "arbitrary")),
        name="mla_flash",
    )(q, k, v)


def _merge_kernel(ya_ref, yb_ref, yc_ref, ga_ref, gb_ref, gc_ref, wa_ref, wb_ref, wc_ref, o_ref):
    def br(y_ref, g_ref, w_ref):
        return jax.nn.sigmoid(g_ref[...].astype(F32)) * jnp.dot(y_ref[...], w_ref[0],
                                                                 preferred_element_type=F32)
    o_ref[...] = (br(ya_ref, ga_ref, wa_ref) + br(yb_ref, gb_ref, wb_ref)
                  + br(yc_ref, gc_ref, wc_ref)).astype(BF16)


def _merge(ya, yb, yc, proj, wa, wb, wc, l, T):
    N, W = ya.shape
    D = wa.shape[2]
    tm = min(1024, T)
    tn = 1024
    ys = pl.BlockSpec((tm, W), lambda i, j: (i, 0))
    gs = lambda k: pl.BlockSpec((tm, tn), lambda i, j: (i, (OFF_GATES + k * D) // tn + j))
    ws = pl.BlockSpec((1, W, tn), lambda i, j: (l, 0, j))
    return pl.pallas_call(
        _merge_kernel,
        grid=(N // tm, D // tn),
        in_specs=[ys, ys, ys, gs(0), gs(1), gs(2), ws, ws, ws],
        out_specs=pl.BlockSpec((tm, tn), lambda i, j: (i, j)),
        out_shape=jax.ShapeDtypeStruct((N, D), BF16),
        compiler_params=_params(("parallel", "arbitrary")),
        name="merge",
    )(ya, yb, yc, proj, proj, proj, wa, wb, wc)


def _outproj_kernel(m_ref, w_ref, x_ref, g1_ref, ng_ref, sc_ref, sh_ref, wr_ref, rb_ref,
                    xo_ref, h_ref, e_ref, rk_ref, wt_ref, cnt_ref, carry_ref, before_ref):
    tm = m_ref.shape[0]

    @pl.when(pl.program_id(0) == 0)
    def _():
        carry_ref[...] = jnp.zeros_like(carry_ref)
        r = lax.broadcasted_iota(I32, (tm, tm), 0)
        c = lax.broadcasted_iota(I32, (tm, tm), 1)
        before_ref[...] = (r < c).astype(BF16)

    x = x_ref[...] + g1_ref[0] * jnp.dot(m_ref[...], w_ref[0], preferred_element_type=F32)
    xo_ref[...] = x
    y = x * lax.rsqrt(jnp.mean(x * x, axis=-1, keepdims=True) + EPS) * ng_ref[...]
    h = y * (1.0 + sc_ref[0]) + sh_ref[0]
    h_ref[...] = h

    h_hi = h.astype(BF16)
    h_lo = (h - h_hi.astype(F32)).astype(BF16)
    hh = jnp.dot(h_hi, wr_ref[...], preferred_element_type=F32)
    logits = hh[:, 0:LANES] + (hh[:, LANES:] + jnp.dot(h_lo, wr_ref[:, 0:LANES], preferred_element_type=F32))
    lt = logits.T
    sc = [jax.nn.sigmoid(lt[r:r + 1, :]) for r in range(N_EXPERTS)]
    bi = [sc[r] + rb_ref[r:r + 1, :] for r in range(N_EXPERTS)]
    gs = []
    for g in range(N_GROUPS):
        b0, b1, b2, b3 = (bi[j * N_GROUPS + g] for j in range(PER_GROUP))
        hi01, lo01 = jnp.maximum(b0, b1), jnp.minimum(b0, b1)
        hi23, lo23 = jnp.maximum(b2, b3), jnp.minimum(b2, b3)
        gs.append(jnp.maximum(hi01, hi23) + jnp.maximum(jnp.minimum(hi01, hi23), jnp.maximum(lo01, lo23)))
    gmax = jnp.maximum(jnp.maximum(gs[0], gs[1]), jnp.maximum(gs[2], gs[3]))
    best = jnp.where(gs[0] == gmax, 0, jnp.where(gs[1] == gmax, 1, jnp.where(gs[2] == gmax, 2, 3)))

    def pick(vals, idx):
        return jnp.where(idx == 0, vals[0], jnp.where(idx == 1, vals[1], jnp.where(idx == 2, vals[2], vals[3])))

    v = [pick([bi[j * N_GROUPS + g] for g in range(N_GROUPS)], best) for j in range(PER_GROUP)]
    s = [pick([sc[j * N_GROUPS + g] for g in range(N_GROUPS)], best) for j in range(PER_GROUP)]

    def first_max(vals):
        mx = jnp.maximum(jnp.maximum(vals[0], vals[1]), jnp.maximum(vals[2], vals[3]))
        return jnp.where(vals[0] == mx, 0, jnp.where(vals[1] == mx, 1, jnp.where(vals[2] == mx, 2, 3)))

    i1 = first_max(v)
    i2 = first_max([jnp.where(i1 == j, -jnp.inf, v[j]) for j in range(PER_GROUP)])
    s1, s2 = pick(s, i1), pick(s, i2)
    tot = s1 + s2
    e1 = best * PER_GROUP + i1
    e2 = best * PER_GROUP + i2
    e_ref[0:1, :] = e1
    e_ref[1:2, :] = e2

    eio = lax.broadcasted_iota(I32, (N_EXPERTS, tm), 0)
    hit1, hit2 = eio == e1, eio == e2
    cnt = (hit1 | hit2).astype(BF16)
    run = jnp.dot(cnt, before_ref[...], preferred_element_type=F32) + carry_ref[:, 0:1]
    rk_ref[0:1, :] = jnp.sum(jnp.where(hit1, run, 0.0), axis=0, keepdims=True).astype(I32)
    rk_ref[1:2, :] = jnp.sum(jnp.where(hit2, run, 0.0), axis=0, keepdims=True).astype(I32)
    carry_ref[...] = carry_ref[...] + jnp.sum(cnt.astype(F32), axis=1, keepdims=True)
    cnt_ref[...] = carry_ref[...]

    rid = lax.broadcasted_iota(I32, (LANES, tm), 0)
    wrow = jnp.where(rid == 0, s1 / tot, jnp.where(rid == 1, s2 / tot, 0.0))
    wt_ref[...] = wrow.T[:, 0:8]


def _outproj(merged, w_out, l, x2, mod, norm_g, wr, rb, T):
    N, D = x2.shape
    tm = min(512, T)
    tpb = T // tm
    mspec = lambda k: pl.BlockSpec((1, 1, D), lambda i: ((i // tpb) * N_MOD + k, 0, 0))
    return pl.pallas_call(
        _outproj_kernel,
        grid=(N // tm,),
        in_specs=[pl.BlockSpec((tm, D), lambda i: (i, 0)),
                  pl.BlockSpec((1, D, D), lambda i: (l, 0, 0), pipeline_mode=pl.Buffered(1)),
                  pl.BlockSpec((tm, D), lambda i: (i, 0)),
                  mspec(2),
                  pl.BlockSpec((1, D), lambda i: (0, 0)),
                  mspec(4), mspec(3),
                  pl.BlockSpec((D, 2 * LANES), lambda i: (0, 0)),
                  pl.BlockSpec((N_EXPERTS, 1), lambda i: (0, 0))],
        out_specs=[pl.BlockSpec((tm, D), lambda i: (i, 0)),
                   pl.BlockSpec((tm, D), lambda i: (i, 0)),
                   pl.BlockSpec((2, tm), lambda i: (0, i)),
                   pl.BlockSpec((2, tm), lambda i: (0, i)),
                   pl.BlockSpec((tm, 8), lambda i: (i, 0)),
                   pl.BlockSpec((N_EXPERTS, LANES), lambda i: (0, 0))],
        out_shape=[jax.ShapeDtypeStruct((N, D), F32),
                   jax.ShapeDtypeStruct((N, D), F32),
                   jax.ShapeDtypeStruct((2, N), I32),
                   jax.ShapeDtypeStruct((2, N), I32),
                   jax.ShapeDtypeStruct((N, 8), F32),
                   jax.ShapeDtypeStruct((N_EXPERTS, LANES), F32)],
        scratch_shapes=[pltpu.VMEM((N_EXPERTS, LANES), F32), pltpu.VMEM((tm, tm), BF16)],
        compiler_params=_params(("arbitrary",)),
        name="outproj_router",
    )(merged, w_out, x2, mod, norm_g.reshape(1, D), mod, mod, wr, rb)


DMA_UNROLL = 8


def _dispatch_kernel(zb_ref, dest_ref, h_ref, xs_ref, zero_ref, sem, zsem):
    tm = h_ref.shape[0]

    @pl.when(pl.program_id(0) == 0)
    def _():
        zero_ref[...] = jnp.zeros_like(zero_ref)

        def zero_copy(n):
            start = pl.multiple_of(zb_ref[n] * MOE_ROWS, MOE_ROWS)
            return pltpu.make_async_copy(zero_ref, xs_ref.at[pl.ds(start, MOE_ROWS), :], zsem)

        for n in range(2 * N_EXPERTS):
            @pl.when(zb_ref[n] >= 0)
            def _():
                zero_copy(n).start()
        for n in range(2 * N_EXPERTS):
            @pl.when(zb_ref[n] >= 0)
            def _():
                zero_copy(n).wait()

    def row_copy(t, k):
        return pltpu.make_async_copy(h_ref.at[pl.ds(t, 1), :], xs_ref.at[pl.ds(dest_ref[k * tm + t], 1), :], sem)

    def issue(t, carry):
        row_copy(t, 0).start()
        row_copy(t, 1).start()
        return carry

    def drain(t, carry):
        row_copy(t, 0).wait()
        row_copy(t, 1).wait()
        return carry

    lax.fori_loop(0, tm, issue, 0, unroll=DMA_UNROLL)
    lax.fori_loop(0, tm, drain, 0, unroll=DMA_UNROLL)


def _row_tile(T):
    return min(512, T)


def _tile_dest(dest, T):
    tm = _row_tile(T)
    return dest.reshape(2, -1, tm).transpose(1, 0, 2).reshape(-1)


def _dispatch(zero_blocks, dest, h, n_rows, T):
    N, D = h.shape
    tm = _row_tile(T)
    return pl.pallas_call(
        _dispatch_kernel,
        grid_spec=pltpu.PrefetchScalarGridSpec(
            num_scalar_prefetch=1,
            grid=(N // tm,),
            in_specs=[pl.BlockSpec((2 * tm,), lambda i, zb: (i,), memory_space=pltpu.SMEM),
                      pl.BlockSpec((tm, D), lambda i, zb: (i, 0))],
            out_specs=pl.BlockSpec(memory_space=pl.ANY),
            scratch_shapes=[pltpu.VMEM((MOE_ROWS, D), F32), pltpu.SemaphoreType.DMA(()),
                            pltpu.SemaphoreType.DMA(())]),
        out_shape=jax.ShapeDtypeStruct((n_rows, D), F32),
        compiler_params=_params(("arbitrary",)),
        name="moe_dispatch",
    )(zero_blocks, dest, h)


def _experts_kernel(be_ref, nb_ref, x_ref, wg_ref, wu_ref, wd_ref, y_ref):
    @pl.when(pl.program_id(0) < nb_ref[0])
    def _():
        x = x_ref[...].astype(BF16)
        a = jnp.dot(x, wg_ref[0, 0], preferred_element_type=F32)
        b = jnp.dot(x, wu_ref[0, 0], preferred_element_type=F32)
        hmid = (a * jax.nn.sigmoid(a) * b).astype(BF16)
        y_ref[...] = jnp.dot(hmid, wd_ref[0, 0], preferred_element_type=F32)

    @pl.when(pl.program_id(0) >= nb_ref[0])
    def _():
        y_ref[...] = jnp.zeros_like(y_ref)


def _experts(blk_e, n_used, xs, wg, wu, wd, l):
    R, D = xs.shape
    F = wg.shape[3]
    nb = R // MOE_ROWS
    row = lambda i, be, nu: (jnp.minimum(i, nu[0] - 1), 0)
    return pl.pallas_call(
        _experts_kernel,
        grid_spec=pltpu.PrefetchScalarGridSpec(
            num_scalar_prefetch=2,
            grid=(nb,),
            in_specs=[pl.BlockSpec((MOE_ROWS, D), row),
                      pl.BlockSpec((1, 1, D, F), lambda i, be, nu: (l, be[i], 0, 0)),
                      pl.BlockSpec((1, 1, D, F), lambda i, be, nu: (l, be[i], 0, 0)),
                      pl.BlockSpec((1, 1, F, D), lambda i, be, nu: (l, be[i], 0, 0))],
            out_specs=pl.BlockSpec((MOE_ROWS, D), lambda i, be, nu: (i, 0))),
        out_shape=jax.ShapeDtypeStruct((R, D), F32),
        compiler_params=_params(("arbitrary",)),
        name="moe_experts",
    )(blk_e, n_used, xs, wg, wu, wd)


def _combine_kernel(dest_ref, y_ref, wt_ref, x_ref, g2_ref, fg_ref, o_ref, buf_ref, sem, *, final):
    tm = x_ref.shape[0]

    def row_copy(t, k):
        return pltpu.make_async_copy(y_ref.at[pl.ds(dest_ref[k * tm + t], 1), :], buf_ref.at[k, pl.ds(t, 1), :], sem)

    def issue(t, carry):
        row_copy(t, 0).start()
        row_copy(t, 1).start()
        return carry

    def drain(t, carry):
        row_copy(t, 0).wait()
        row_copy(t, 1).wait()
        return carry

    lax.fori_loop(0, tm, issue, 0, unroll=DMA_UNROLL)
    lax.fori_loop(0, tm, drain, 0, unroll=DMA_UNROLL)
    wt = wt_ref[...]
    moe = buf_ref[0] * wt[:, 0:1] + buf_ref[1] * wt[:, 1:2]
    x = x_ref[...] + g2_ref[0] * moe
    if final:
        x = x * lax.rsqrt(jnp.mean(x * x, axis=-1, keepdims=True) + EPS) * fg_ref[...]
    o_ref[...] = x


def _combine(dest, y, wt, x2, mod, final_g, T, final):
    N, D = x2.shape
    tm = _row_tile(T)
    tpb = T // tm
    return pl.pallas_call(
        functools.partial(_combine_kernel, final=final),
        grid=(N // tm,),
        in_specs=[pl.BlockSpec((2 * tm,), lambda i: (i,), memory_space=pltpu.SMEM),
                  pl.BlockSpec(memory_space=pl.ANY),
                  pl.BlockSpec((tm, 8), lambda i: (i, 0)),
                  pl.BlockSpec((tm, D), lambda i: (i, 0)),
                  pl.BlockSpec((1, 1, D), lambda i: ((i // tpb) * N_MOD + 5, 0, 0)),
                  pl.BlockSpec((1, D), lambda i: (0, 0))],
        out_specs=pl.BlockSpec((tm, D), lambda i: (i, 0)),
        out_shape=jax.ShapeDtypeStruct((N, D), F32),
        scratch_shapes=[pltpu.VMEM((2, tm, D), F32), pltpu.SemaphoreType.DMA(())],
        compiler_params=_params(("arbitrary",)),
        name="moe_combine",
    )(dest, y, wt, x2, mod, final_g.reshape(1, D))


def _pack_w_in(w):
    L, D, _ = w.shape
    widths = (GM_WIDTH, GM_WIDTH, 3 * DN_WIDTH, DN_WIDTH, DN_HEADS, DN_HEADS, ML_RANK, ML_RANK, ML_ROPE,
              3 * D)
    offs = [0]
    for wd in widths:
        offs.append(offs[-1] + wd)
    u, v, qkv, z, b, a, cq, ckv, kr, gates = (w[:, :, offs[i]:offs[i + 1]] for i in range(len(widths)))
    kr_rot = jnp.concatenate([-kr[:, :, ML_ROPE // 2:], kr[:, :, :ML_ROPE // 2]], axis=2)
    zeros = jnp.zeros((L, D, PROJ_COLS - OFF_BA - 2 * DN_HEADS), F32)
    return jnp.concatenate([u, v, qkv, z, gates, cq, ckv, kr, kr_rot, b, a, zeros], axis=2).astype(BF16)


def _block_table(counts, n_blocks):
    per = (counts + MOE_ROWS - 1) // MOE_ROWS
    ends = jnp.cumsum(per)
    first_row = (ends - per) * MOE_ROWS
    ends_le = ends[None, :] <= jnp.arange(n_blocks, dtype=I32)[:, None]
    blk_e = jnp.minimum(jnp.sum(ends_le.astype(I32), axis=1), N_EXPERTS - 1)
    n_used = ends[-1]
    tail = n_used + jnp.arange(N_EXPERTS, dtype=I32)
    zero_blocks = jnp.concatenate([jnp.where(per > 0, ends - 1, -1), jnp.where(tail < n_blocks, tail, -1)])
    return blk_e, n_used.reshape(1).astype(I32), first_row.astype(I32), zero_blocks.astype(I32)


def kernel(x, c, positions, w_ada, b_ada, norm1_g, norm2_g, w_in, gmlp_ln_g, gmlp_ln_b, gmlp_ws, gmlp_bs, dn_conv_w, dn_a_log, dn_dt_bias, dn_norm_g, mla_cq_g, mla_ckv_g, mla_w_uq, mla_w_ukv, w_br_gmlp, w_br_dn, w_br_mla, w_out, w_router, router_bias, w_gate, w_up, w_down, final_norm_g):
    B, T, D = x.shape
    L = w_in.shape[0]
    N = B * T
    x2 = x.reshape(N, D)
    pos_col = positions.reshape(N, 1)
    mod_all = _ada_mod(c, w_ada, b_ada).reshape(L, B * N_MOD, 1, D)

    perm = jnp.array([(r % N_GROUPS) * PER_GROUP + r // N_GROUPS for r in range(N_EXPERTS)], I32)
    wr = jnp.concatenate([w_router[:, perm], jnp.zeros((D, LANES - N_EXPERTS), F32)], axis=1)
    wr_hi = wr.astype(BF16)
    wr_hl = jnp.concatenate([wr_hi, (wr - wr_hi.astype(F32)).astype(BF16)], axis=1)
    rb = router_bias[perm].reshape(N_EXPERTS, 1)
    n_blocks = (2 * N) // MOE_ROWS + N_EXPERTS

    w_in_p = _pack_w_in(w_in)
    wq_p, wkv_p = _pack_mla_weights(mla_w_uq, mla_w_ukv)
    w_a, w_b, w_c, w_o = (w.astype(BF16) for w in (w_br_gmlp, w_br_dn, w_br_mla, w_out))
    w_g, w_u, w_d = (w.astype(BF16) for w in (w_gate, w_up, w_down))
    rope_cs = _rope_tables(pos_col, T)

    for l in range(L):
        mod = mod_all[l]
        proj = _inproj(x2, norm1_g[l], mod, w_in_p, l, T)
        ya = _gmlp(proj, gmlp_ln_g[l], gmlp_ln_b[l], gmlp_ws[l], gmlp_bs[l], T)
        yb = _deltanet(proj, dn_conv_w[l], dn_a_log[l], dn_dt_bias[l], dn_norm_g[l], B, T)
        q, k, v = _mla_proj(proj, rope_cs, mla_cq_g[l], mla_ckv_g[l], wq_p, wkv_p, l, B, T)
        yc = _flash(q, k, v).reshape(N, ML_HEADS * ML_V)
        merged = _merge(ya, yb, yc, proj, w_a, w_b, w_c, l, T)
        x2, h2, e, rank, wt, counts = _outproj(merged, w_o, l, x2, mod, norm2_g[l], wr_hl, rb, T)
        blk_e, n_used, first_row, zero_blocks = _block_table(counts[:, 0].astype(I32), n_blocks)
        onehot = e[:, :, None] == jnp.arange(N_EXPERTS, dtype=I32)
        dest = _tile_dest(rank + jnp.sum(jnp.where(onehot, first_row, 0), axis=-1), T)
        xs = _dispatch(zero_blocks, dest, h2, n_blocks * MOE_ROWS, T)
        y = _experts(blk_e, n_used, xs, w_g, w_u, w_d, l)
        x2 = _combine(dest, y, wt, x2, mod, final_norm_g, T, final=(l == L - 1))
    return x2.reshape(B, T, D)
```

```python
import functools
import math

import jax
import jax.numpy as jnp
from jax import lax
from jax.experimental import pallas as pl
from jax.experimental.pallas import tpu as pltpu

F32 = jnp.float32
BF16 = jnp.bfloat16
I32 = jnp.int32

EPS = 1e-6
N_MOD = 6
GM_GROUPS, GM_DIM, GM_CHUNK = 8, 128, 128
GM_WIDTH = GM_GROUPS * GM_DIM
DN_HEADS, DN_DIM, DN_CONV = 8, 128, 4
DN_WIDTH = DN_HEADS * DN_DIM
DN_CHUNK = 128
DN_BASE = 16
ML_HEADS, ML_RANK, ML_NOPE, ML_ROPE, ML_V = 8, 512, 128, 64, 128
ML_QK = ML_NOPE + ML_ROPE
ROPE_THETA = 10000.0
N_EXPERTS, N_GROUPS, PER_GROUP, D_EXPERT = 16, 4, 4, 1024
MOE_ROWS = 256

LANES = 128
OFF_U, OFF_V, OFF_Q, OFF_K, OFF_VV, OFF_Z = 0, 1024, 2048, 3072, 4096, 5120
OFF_GATES = 6144
OFF_CQ = OFF_GATES + 3 * 2048
OFF_CKV = OFF_CQ + ML_RANK
OFF_KR = OFF_CKV + ML_RANK
OFF_BA = OFF_KR + LANES
PROJ_TN = 1536
PROJ_COLS = 13824

VMEM_LIMIT = 52 * 1024 * 1024


def _params(sem, vmem=VMEM_LIMIT):
    return pltpu.CompilerParams(dimension_semantics=sem, vmem_limit_bytes=vmem)


def _ada_kernel(c_ref, w_ref, b_ref, o_ref):
    c = c_ref[...]
    ca = (c * jax.nn.sigmoid(c)).astype(BF16)
    o_ref[0] = jnp.dot(ca, w_ref[0].astype(BF16), preferred_element_type=F32) + b_ref[0]


def _ada_mod(c, w_ada, b_ada):
    L, D, M = w_ada.shape
    B = c.shape[0]
    tn = 1024
    return pl.pallas_call(
        _ada_kernel,
        grid=(L, M // tn),
        in_specs=[pl.BlockSpec((B, D), lambda l, j: (0, 0)),
                  pl.BlockSpec((1, D, tn), lambda l, j: (l, 0, j)),
                  pl.BlockSpec((1, 1, tn), lambda l, j: (l, 0, j))],
        out_specs=pl.BlockSpec((1, B, tn), lambda l, j: (l, 0, j)),
        out_shape=jax.ShapeDtypeStruct((L, B, M), F32),
        compiler_params=_params(("parallel", "parallel")),
        name="ada_mod",
    )(c, w_ada, b_ada.reshape(L, 1, M))


def _inproj_kernel(x_ref, g_ref, sc_ref, sh_ref, w_ref, o_ref, h_ref):
    @pl.when(pl.program_id(1) == 0)
    def _():
        x = x_ref[...]
        y = x * lax.rsqrt(jnp.mean(x * x, axis=-1, keepdims=True) + EPS) * g_ref[...]
        h_ref[...] = (y * (1.0 + sc_ref[0]) + sh_ref[0]).astype(BF16)

    o_ref[...] = jnp.dot(h_ref[...], w_ref[0], preferred_element_type=F32).astype(BF16)


def _inproj(x2, g, mod, w, l, T):
    N, D = x2.shape
    C = w.shape[2]
    tm = min(1024, T)
    tpb = T // tm
    return pl.pallas_call(
        _inproj_kernel,
        grid=(N // tm, C // PROJ_TN),
        in_specs=[pl.BlockSpec((tm, D), lambda i, j: (i, 0)),
                  pl.BlockSpec((1, D), lambda i, j: (0, 0)),
                  pl.BlockSpec((1, 1, D), lambda i, j: ((i // tpb) * N_MOD + 1, 0, 0)),
                  pl.BlockSpec((1, 1, D), lambda i, j: ((i // tpb) * N_MOD + 0, 0, 0)),
                  pl.BlockSpec((1, D, PROJ_TN), lambda i, j: (l, 0, j))],
        out_specs=pl.BlockSpec((tm, PROJ_TN), lambda i, j: (i, j)),
        out_shape=jax.ShapeDtypeStruct((N, C), BF16),
        scratch_shapes=[pltpu.VMEM((tm, D), BF16)],
        compiler_params=_params(("parallel", "arbitrary")),
        name="inproj",
    )(x2, g.reshape(1, D), mod, mod, w)


def _gmlp_kernel(u_ref, v_ref, lg_ref, lb_ref, ws_ref, bs_ref, o_ref):
    tm = u_ref.shape[0]
    v = jax.nn.gelu(v_ref[...].astype(F32))
    mu = jnp.mean(v, axis=-1, keepdims=True)
    vc = v - mu
    var = jnp.mean(vc * vc, axis=-1, keepdims=True)
    vn = (vc * lax.rsqrt(var + EPS) * lg_ref[...] + lb_ref[...]).astype(BF16)
    row = lax.broadcasted_iota(I32, (GM_CHUNK, GM_CHUNK), 0)
    col = lax.broadcasted_iota(I32, (GM_CHUNK, GM_CHUNK), 1)
    causal = col <= row
    for g in range(GM_GROUPS):
        w = jnp.where(causal, ws_ref[g], 0.0).astype(BF16)
        lo, hi = g * GM_DIM, (g + 1) * GM_DIM
        for c in range(tm // GM_CHUNK):
            r0, r1 = c * GM_CHUNK, (c + 1) * GM_CHUNK
            s = jnp.dot(w, vn[r0:r1, lo:hi], preferred_element_type=F32) + bs_ref[g]
            u = jax.nn.gelu(u_ref[r0:r1, lo:hi].astype(F32))
            o_ref[r0:r1, lo:hi] = (u * s).astype(BF16)


def _gmlp(proj, ln_g, ln_b, ws, bs, T):
    N = proj.shape[0]
    tm = min(512, T)
    W = GM_WIDTH
    return pl.pallas_call(
        _gmlp_kernel,
        grid=(N // tm,),
        in_specs=[pl.BlockSpec((tm, W), lambda i: (i, OFF_U // W)),
                  pl.BlockSpec((tm, W), lambda i: (i, OFF_V // W)),
                  pl.BlockSpec((1, W), lambda i: (0, 0)),
                  pl.BlockSpec((1, W), lambda i: (0, 0)),
                  pl.BlockSpec((GM_GROUPS, GM_CHUNK, GM_CHUNK), lambda i: (0, 0, 0)),
                  pl.BlockSpec((GM_GROUPS, GM_CHUNK, 1), lambda i: (0, 0, 0))],
        out_specs=pl.BlockSpec((tm, W), lambda i: (i, 0)),
        out_shape=jax.ShapeDtypeStruct((N, W), BF16),
        compiler_params=_params(("parallel",)),
        name="gmlp",
    )(proj, proj, ln_g.reshape(1, W), ln_b.reshape(1, W), ws, bs.reshape(GM_GROUPS, GM_CHUNK, 1))


def _mm(a, b):
    return jnp.dot(a.astype(BF16), b.astype(BF16), preferred_element_type=F32)


def _unit_lower_inverses(lows):
    C = lows[0].shape[0]
    n = range(len(lows))
    r = lax.broadcasted_iota(I32, (C, C), 0)
    c = lax.broadcasted_iota(I32, (C, C), 1)
    eye = (r == c).astype(F32)
    s = DN_BASE
    diag = (r // s) == (c // s)
    p = [-jnp.where(diag, lows[i], 0.0) for i in n]
    t = [eye + p[i] for i in n]
    for _ in range(int(math.log2(s)) - 1):
        p = [_mm(p[i], p[i]) for i in n]
        tp = [_mm(t[i], p[i]) for i in n]
        t = [t[i] + tp[i] for i in n]
    while s < C:
        band = ((r // (2 * s)) == (c // (2 * s))) & ((r // s) != (c // s))
        ot = [_mm(jnp.where(band, lows[i], 0.0), t[i]) for i in n]
        tot = [_mm(t[i], ot[i]) for i in n]
        t = [t[i] - tot[i] for i in n]
        s *= 2
    return t


def _dn_kernel(q_ref, k_ref, v_ref, z_ref, ba_ref, cw_ref, alog_ref, dtb_ref, ng_ref, o_ref,
               ext_ref, st_ref):
    tc = q_ref.shape[0]
    C = DN_CHUNK

    @pl.when(pl.program_id(1) == 0)
    def _():
        ext_ref[:, 0:8, :] = jnp.zeros((3, 8, DN_WIDTH), F32)
        st_ref[...] = jnp.zeros_like(st_ref)

    @pl.when(pl.program_id(1) != 0)
    def _():
        ext_ref[:, 0:8, :] = ext_ref[:, tc:tc + 8, :]

    ext_ref[0, 8:, :] = q_ref[...].astype(F32)
    ext_ref[1, 8:, :] = k_ref[...].astype(F32)
    ext_ref[2, 8:, :] = v_ref[...].astype(F32)

    def conv_silu(i):
        acc = jnp.zeros((tc, DN_WIDTH), F32)
        for kk in range(DN_CONV):
            acc = acc + cw_ref[i, kk:kk + 1, :] * ext_ref[i, pl.ds(8 - (DN_CONV - 1) + kk, tc), :]
        return acc * jax.nn.sigmoid(acc)

    qa, ka, va = conv_silu(0), conv_silu(1), conv_silu(2)

    ba = ba_ref[...].astype(F32)
    beta_all = jax.nn.sigmoid(ba)
    g_all = -jnp.exp(alog_ref[...]) * jax.nn.softplus(ba + dtb_ref[...])
    pos = lax.broadcasted_iota(I32, (tc, LANES), 0) % C
    gc_all = g_all
    sh = 1
    while sh < C:
        gc_all = gc_all + jnp.where(pos >= sh, pltpu.roll(gc_all, sh, 0), 0.0)
        sh *= 2
    gc_rows = gc_all.T

    r = lax.broadcasted_iota(I32, (C, C), 0)
    c = lax.broadcasted_iota(I32, (C, C), 1)
    incl = c <= r
    strict = c < r

    H = DN_HEADS
    nc = tc // C
    idx = [(ci, h) for ci in range(nc) for h in range(H)]
    G = range(len(idx))

    def piece(x, ci, h):
        return x[ci * C:(ci + 1) * C, h * DN_DIM:(h + 1) * DN_DIM]

    def l2n(x):
        return x * lax.rsqrt(jnp.sum(x * x, axis=-1, keepdims=True) + EPS)

    q = [l2n(piece(qa, ci, h)) * (DN_DIM ** -0.5) for ci, h in idx]
    k = [l2n(piece(ka, ci, h)) for ci, h in idx]
    v = [piece(va, ci, h) for ci, h in idx]
    beta = [beta_all[ci * C:(ci + 1) * C, h:h + 1] for ci, h in idx]
    g_c = [gc_all[ci * C:(ci + 1) * C, H + h:H + h + 1] for ci, h in idx]
    g_r = [gc_rows[H + h:H + h + 1, ci * C:(ci + 1) * C] for ci, h in idx]
    g_last = [g_r[i][:, C - 1:C] for i in G]
    decay = [jnp.exp(jnp.where(incl, g_c[i] - g_r[i], -1e30)) for i in G]
    kb = [k[i] * beta[i] for i in G]
    kt = [k[i].T for i in G]
    ktb = [kt[i].astype(BF16) for i in G]
    kk = [jnp.dot(kb[i].astype(BF16), ktb[i], preferred_element_type=F32) for i in G]
    qk = [jnp.dot(q[i].astype(BF16), ktb[i], preferred_element_type=F32) for i in G]
    tinv = _unit_lower_inverses([jnp.where(strict, kk[i] * decay[i], 0.0) for i in G])
    intra = [qk[i] * decay[i] for i in G]
    eg = [jnp.exp(g_c[i]) for i in G]
    u = [_mm(tinv[i], v[i] * beta[i]) for i in G]
    w = [_mm(tinv[i], kb[i] * eg[i]) for i in G]
    qg = [q[i] * eg[i] for i in G]
    k_dec_t = [kt[i] * jnp.exp(g_last[i] - g_r[i]) for i in G]

    state = [st_ref[h] for h in range(H)]
    for ci in range(nc):
        ids = [ci * H + h for h in range(H)]
        ws = [_mm(w[i], state[h]) for h, i in enumerate(ids)]
        qs = [_mm(qg[i], state[h]) for h, i in enumerate(ids)]
        v_new = [u[i] - ws[h] for h, i in enumerate(ids)]
        av = [_mm(intra[i], v_new[h]) for h, i in enumerate(ids)]
        kv = [_mm(k_dec_t[i], v_new[h]) for h, i in enumerate(ids)]
        state = [state[h] * jnp.exp(g_last[i]) + kv[h] for h, i in enumerate(ids)]
        for h in range(H):
            o_c = qs[h] + av[h]
            o_n = o_c * lax.rsqrt(jnp.mean(o_c * o_c, axis=-1, keepdims=True) + EPS) * ng_ref[...]
            z = piece(z_ref, ci, h).astype(F32)
            o_ref[ci * C:(ci + 1) * C, h * DN_DIM:(h + 1) * DN_DIM] = (o_n * (z * jax.nn.sigmoid(z))).astype(BF16)
    for h in range(H):
        st_ref[h] = state[h]


def _deltanet(proj, conv_w, a_log, dt_bias, norm_g, B, T):
    N = proj.shape[0]
    tc = min(256, T)
    tpb = T // tc
    W = DN_WIDTH
    cw = conv_w.reshape(DN_CONV, 3, W).transpose(1, 0, 2)
    pad = jnp.zeros((LANES - 2 * DN_HEADS,), F32)
    alog = jnp.concatenate([jnp.zeros((DN_HEADS,), F32), a_log, pad]).reshape(1, LANES)
    dtb = jnp.concatenate([jnp.zeros((DN_HEADS,), F32), dt_bias, pad]).reshape(1, LANES)
    blk = lambda off: pl.BlockSpec((tc, W), lambda b, t: (b * tpb + t, off // W))
    return pl.pallas_call(
        _dn_kernel,
        grid=(B, tpb),
        in_specs=[blk(OFF_Q), blk(OFF_K), blk(OFF_VV), blk(OFF_Z),
                  pl.BlockSpec((tc, LANES), lambda b, t: (b * tpb + t, OFF_BA // LANES)),
                  pl.BlockSpec((3, DN_CONV, W), lambda b, t: (0, 0, 0)),
                  pl.BlockSpec((1, LANES), lambda b, t: (0, 0)),
                  pl.BlockSpec((1, LANES), lambda b, t: (0, 0)),
                  pl.BlockSpec((1, DN_DIM), lambda b, t: (0, 0))],
        out_specs=pl.BlockSpec((tc, W), lambda b, t: (b * tpb + t, 0)),
        out_shape=jax.ShapeDtypeStruct((N, W), BF16),
        scratch_shapes=[pltpu.VMEM((3, tc + 8, W), F32),
                        pltpu.VMEM((DN_HEADS, DN_DIM, DN_DIM), F32)],
        compiler_params=_params(("parallel", "arbitrary")),
        name="deltanet",
    )(proj, proj, proj, proj, proj, cw, alog, dtb, norm_g.reshape(1, DN_DIM))


def _rope_kernel(pos_ref, invf_ref, o_ref):
    ang = pos_ref[...].astype(F32) * invf_ref[...]
    o_ref[0] = jnp.cos(ang)
    o_ref[1] = jnp.sin(ang)


def _rope_tables(pos_col, T):
    N = pos_col.shape[0]
    tm = min(1024, T)
    half = ROPE_THETA ** (-jnp.arange(0, ML_ROPE, 2, dtype=F32) / ML_ROPE)
    invf = jnp.tile(half, LANES // (ML_ROPE // 2)).reshape(1, LANES)
    return pl.pallas_call(
        _rope_kernel,
        grid=(N // tm,),
        in_specs=[pl.BlockSpec((tm, 1), lambda i: (i, 0)),
                  pl.BlockSpec((1, LANES), lambda i: (0, 0))],
        out_specs=pl.BlockSpec((2, tm, LANES), lambda i: (0, i, 0)),
        out_shape=jax.ShapeDtypeStruct((2, N, LANES), F32),
        compiler_params=_params(("parallel",)),
        name="rope_tables",
    )(pos_col, invf)


def _mla_proj_kernel(cq_ref, ckv_ref, kr_ref, cs_ref, gq_ref, gkv_ref, wq_ref, wkv_ref,
                     q_ref, k_ref, v_ref):
    tm = cq_ref.shape[0]
    cos, sin = cs_ref[0], cs_ref[1]
    scale = ML_QK ** -0.5 * math.log2(math.e)

    def rms(x, g):
        return (x * lax.rsqrt(jnp.mean(x * x, axis=-1, keepdims=True) + EPS) * g).astype(BF16)

    def rope(pair):
        return pair * cos + pltpu.roll(pair, ML_ROPE, 1) * sin

    cqn = rms(cq_ref[...].astype(F32), gq_ref[...])
    ckvn = rms(ckv_ref[...].astype(F32), gkv_ref[...])
    k_rope = rope(kr_ref[...].astype(F32))[:, 0:ML_ROPE].astype(BF16)
    ones = jnp.ones((tm, ML_V), BF16)
    W2 = 2 * LANES
    for h in range(ML_HEADS):
        qa = jnp.dot(cqn, wq_ref[0, :, h * W2:(h + 1) * W2], preferred_element_type=F32)
        q_ref[0, h, :, 0:ML_NOPE] = (qa[:, 0:ML_NOPE] * scale).astype(BF16)
        q_ref[0, h, :, ML_NOPE:ML_QK] = (rope(qa[:, ML_NOPE:]) * scale)[:, 0:ML_ROPE].astype(BF16)
        kva = jnp.dot(ckvn, wkv_ref[0, :, h * W2:(h + 1) * W2], preferred_element_type=F32)
        k_ref[0, h, :, 0:ML_NOPE] = kva[:, 0:ML_NOPE].astype(BF16)
        k_ref[0, h, :, ML_NOPE:ML_QK] = k_rope
        v_ref[0, h, :, 0:ML_V] = kva[:, ML_NOPE:].astype(BF16)
        v_ref[0, h, :, ML_V:] = ones


def _pack_mla_weights(w_uq, w_ukv):
    L = w_uq.shape[0]
    H, R = ML_HEADS, ML_RANK
    wq = w_uq.reshape(L, R, H, ML_QK)
    wr = wq[..., ML_NOPE:]
    wq = jnp.concatenate([wq, -wr[..., ML_ROPE // 2:], wr[..., :ML_ROPE // 2]], axis=-1)
    return wq.reshape(L, R, H * 2 * LANES).astype(BF16), w_ukv.astype(BF16)


def _mla_proj(proj, rope_cs, cq_g, ckv_g, wq, wkv, l, B, T):
    N = proj.shape[0]
    H, R = ML_HEADS, ML_RANK
    tm = min(512, T)
    tpb = T // tm
    hs = lambda d: pl.BlockSpec((1, H, tm, d), lambda i: (i // tpb, 0, i % tpb, 0))
    ws = pl.BlockSpec((1, R, H * 2 * LANES), lambda i: (l, 0, 0))
    return pl.pallas_call(
        _mla_proj_kernel,
        grid=(N // tm,),
        in_specs=[pl.BlockSpec((tm, R), lambda i: (i, OFF_CQ // R)),
                  pl.BlockSpec((tm, R), lambda i: (i, OFF_CKV // R)),
                  pl.BlockSpec((tm, LANES), lambda i: (i, OFF_KR // LANES)),
                  pl.BlockSpec((2, tm, LANES), lambda i: (0, i, 0)),
                  pl.BlockSpec((1, R), lambda i: (0, 0)),
                  pl.BlockSpec((1, R), lambda i: (0, 0)),
                  ws, ws],
        out_specs=[hs(ML_QK), hs(ML_QK), hs(2 * ML_V)],
        out_shape=[jax.ShapeDtypeStruct((B, H, T, ML_QK), BF16),
                   jax.ShapeDtypeStruct((B, H, T, ML_QK), BF16),
                   jax.ShapeDtypeStruct((B, H, T, 2 * ML_V), BF16)],
        compiler_params=_params(("parallel",)),
        name="mla_proj",
    )(proj, proj, proj, rope_cs, cq_g.reshape(1, R), ckv_g.reshape(1, R), wq, wkv)


def _flash_kernel(q_ref, k_ref, v_ref, o_ref, m_ref, l_ref, acc_ref):
    hp, tq = q_ref.shape[1], q_ref.shape[2]
    heads = range(hp)
    qi = pl.program_id(2)
    m_ref[...] = jnp.full_like(m_ref, -jnp.inf)
    l_ref[...] = jnp.zeros_like(l_ref)
    acc_ref[...] = jnp.zeros_like(acc_ref)

    def step(row0, nr, start, nc, causal):
        rows = pl.ds(row0, nr)
        keys = pl.ds(start, nc)
        s = [lax.dot_general(q_ref[0, h, rows, :], k_ref[0, h, keys, :], (((1,), (1,)), ((), ())),
                             preferred_element_type=F32) for h in heads]
        if causal:
            r = lax.broadcasted_iota(I32, (nr, nc), 0)
            c = lax.broadcasted_iota(I32, (nr, nc), 1)
            s = [jnp.where(c <= r, s[h], -jnp.inf) for h in heads]
        m_prev = [m_ref[h, rows, :] for h in heads]
        m_new = [jnp.maximum(m_prev[h], jnp.max(s[h], axis=-1, keepdims=True)) for h in heads]
        p = [jnp.exp2(s[h] - jnp.concatenate([m_new[h]] * (nc // LANES), axis=1)) for h in heads]
        alpha = [jnp.exp2(m_prev[h] - m_new[h]) for h in heads]
        pv = [jnp.dot(p[h].astype(BF16), v_ref[0, h, keys, :], preferred_element_type=F32) for h in heads]
        for h in heads:
            l_ref[h, rows, :] = alpha[h] * l_ref[h, rows, :] + pv[h][:, ML_V:]
            acc_ref[h, rows, :] = alpha[h] * acc_ref[h, rows, :] + pv[h][:, :ML_V]
            m_ref[h, rows, :] = m_new[h]

    def body(j, carry):
        step(0, tq, pl.multiple_of(j * tq, tq), tq, False)
        return carry

    lax.fori_loop(0, qi, body, 0)
    half = tq // 2
    diag = pl.multiple_of(qi * tq, tq)
    step(0, tq, diag, half, True)
    step(half, half, pl.multiple_of(diag + half, half), half, True)
    for h in heads:
        o_ref[0, :, h * ML_V:(h + 1) * ML_V] = (acc_ref[h] / l_ref[h]).astype(BF16)


FLASH_HEADS = 2


def _flash(q, k, v):
    B, H, T, _ = q.shape
    tq = min(1024, T)
    hp = FLASH_HEADS
    return pl.pallas_call(
        _flash_kernel,
        grid=(B, H // hp, T // tq),
        in_specs=[pl.BlockSpec((1, hp, tq, ML_QK), lambda b, h, i: (b, h, i, 0)),
                  pl.BlockSpec((1, hp, T, ML_QK), lambda b, h, i: (b, h, 0, 0)),
                  pl.BlockSpec((1, hp, T, 2 * ML_V), lambda b, h, i: (b, h, 0, 0))],
        out_specs=pl.BlockSpec((1, tq, hp * ML_V), lambda b, h, i: (b, i, h)),
        out_shape=jax.ShapeDtypeStruct((B, T, H * ML_V), BF16),
        scratch_shapes=[pltpu.VMEM((hp, tq, LANES), F32), pltpu.VMEM((hp, tq, LANES), F32),
                        pltpu.VMEM((hp, tq, ML_V), F32)],
        compiler_params=_params(("parallel", "parallel", "arbitrary")),
        name="mla_flash",
    )(q, k, v)


def _merge_kernel(ya_ref, yb_ref, yc_ref, ga_ref, gb_ref, gc_ref, wa_ref, wb_ref, wc_ref, o_ref):
    def br(y_ref, g_ref, w_ref):
        return jax.nn.sigmoid(g_ref[...].astype(F32)) * jnp.dot(y_ref[...], w_ref[0],
                                                                 preferred_element_type=F32)
    o_ref[...] = (br(ya_ref, ga_ref, wa_ref) + br(yb_ref, gb_ref, wb_ref)
                  + br(yc_ref, gc_ref, wc_ref)).astype(BF16)


def _merge(ya, yb, yc, proj, wa, wb, wc, l, T):
    N, W = ya.shape
    D = wa.shape[2]
    tm = min(1024, T)
    tn = 1024
    ys = pl.BlockSpec((tm, W), lambda i, j: (i, 0))
    gs = lambda k: pl.BlockSpec((tm, tn), lambda i, j: (i, (OFF_GATES + k * D) // tn + j))
    ws = pl.BlockSpec((1, W, tn), lambda i, j: (l, 0, j))
    return pl.pallas_call(
        _merge_kernel,
        grid=(N // tm, D // tn),
        in_specs=[ys, ys, ys, gs(0), gs(1), gs(2), ws, ws, ws],
        out_specs=pl.BlockSpec((tm, tn), lambda i, j: (i, j)),
        out_shape=jax.ShapeDtypeStruct((N, D), BF16),
        compiler_params=_params(("parallel", "arbitrary")),
        name="merge",
    )(ya, yb, yc, proj, proj, proj, wa, wb, wc)


def _outproj_kernel(m_ref, w_ref, x_ref, g1_ref, ng_ref, sc_ref, sh_ref, wr_ref, rb_ref,
                    xo_ref, h_ref, e_ref, rk_ref, wt_ref, cnt_ref, carry_ref, before_ref):
    tm = m_ref.shape[0]

    @pl.when(pl.program_id(0) == 0)
    def _():
        carry_ref[...] = jnp.zeros_like(carry_ref)
        r = lax.broadcasted_iota(I32, (tm, tm), 0)
        c = lax.broadcasted_iota(I32, (tm, tm), 1)
        before_ref[...] = (r < c).astype(BF16)

    x = x_ref[...] + g1_ref[0] * jnp.dot(m_ref[...], w_ref[0], preferred_element_type=F32)
    xo_ref[...] = x
    y = x * lax.rsqrt(jnp.mean(x * x, axis=-1, keepdims=True) + EPS) * ng_ref[...]
    h = y * (1.0 + sc_ref[0]) + sh_ref[0]
    h_ref[...] = h

    h_hi = h.astype(BF16)
    h_lo = (h - h_hi.astype(F32)).astype(BF16)
    hh = jnp.dot(h_hi, wr_ref[...], preferred_element_type=F32)
    logits = hh[:, 0:LANES] + (hh[:, LANES:] + jnp.dot(h_lo, wr_ref[:, 0:LANES], preferred_element_type=F32))
    lt = logits.T
    sc = [jax.nn.sigmoid(lt[r:r + 1, :]) for r in range(N_EXPERTS)]
    bi = [sc[r] + rb_ref[r:r + 1, :] for r in range(N_EXPERTS)]
    gs = []
    for g in range(N_GROUPS):
        b0, b1, b2, b3 = (bi[j * N_GROUPS + g] for j in range(PER_GROUP))
        hi01, lo01 = jnp.maximum(b0, b1), jnp.minimum(b0, b1)
        hi23, lo23 = jnp.maximum(b2, b3), jnp.minimum(b2, b3)
        gs.append(jnp.maximum(hi01, hi23) + jnp.maximum(jnp.minimum(hi01, hi23), jnp.maximum(lo01, lo23)))
    gmax = jnp.maximum(jnp.maximum(gs[0], gs[1]), jnp.maximum(gs[2], gs[3]))
    best = jnp.where(gs[0] == gmax, 0, jnp.where(gs[1] == gmax, 1, jnp.where(gs[2] == gmax, 2, 3)))

    def pick(vals, idx):
        return jnp.where(idx == 0, vals[0], jnp.where(idx == 1, vals[1], jnp.where(idx == 2, vals[2], vals[3])))

    v = [pick([bi[j * N_GROUPS + g] for g in range(N_GROUPS)], best) for j in range(PER_GROUP)]
    s = [pick([sc[j * N_GROUPS + g] for g in range(N_GROUPS)], best) for j in range(PER_GROUP)]

    def first_max(vals):
        mx = jnp.maximum(jnp.maximum(vals[0], vals[1]), jnp.maximum(vals[2], vals[3]))
        return jnp.where(vals[0] == mx, 0, jnp.where(vals[1] == mx, 1, jnp.where(vals[2] == mx, 2, 3)))

    i1 = first_max(v)
    i2 = first_max([jnp.where(i1 == j, -jnp.inf, v[j]) for j in range(PER_GROUP)])
    s1, s2 = pick(s, i1), pick(s, i2)
    tot = s1 + s2
    e1 = best * PER_GROUP + i1
    e2 = best * PER_GROUP + i2
    e_ref[0:1, :] = e1
    e_ref[1:2, :] = e2

    eio = lax.broadcasted_iota(I32, (N_EXPERTS, tm), 0)
    hit1, hit2 = eio == e1, eio == e2
    cnt = (hit1 | hit2).astype(BF16)
    run = jnp.dot(cnt, before_ref[...], preferred_element_type=F32) + carry_ref[:, 0:1]
    rk_ref[0:1, :] = jnp.sum(jnp.where(hit1, run, 0.0), axis=0, keepdims=True).astype(I32)
    rk_ref[1:2, :] = jnp.sum(jnp.where(hit2, run, 0.0), axis=0, keepdims=True).astype(I32)
    carry_ref[...] = carry_ref[...] + jnp.sum(cnt.astype(F32), axis=1, keepdims=True)
    cnt_ref[...] = carry_ref[...]

    rid = lax.broadcasted_iota(I32, (LANES, tm), 0)
    wrow = jnp.where(rid == 0, s1 / tot, jnp.where(rid == 1, s2 / tot, 0.0))
    wt_ref[...] = wrow.T[:, 0:8]


def _outproj(merged, w_out, l, x2, mod, norm_g, wr, rb, T):
    N, D = x2.shape
    tm = min(512, T)
    tpb = T // tm
    mspec = lambda k: pl.BlockSpec((1, 1, D), lambda i: ((i // tpb) * N_MOD + k, 0, 0))
    return pl.pallas_call(
        _outproj_kernel,
        grid=(N // tm,),
        in_specs=[pl.BlockSpec((tm, D), lambda i: (i, 0)),
                  pl.BlockSpec((1, D, D), lambda i: (l, 0, 0), pipeline_mode=pl.Buffered(1)),
                  pl.BlockSpec((tm, D), lambda i: (i, 0)),
                  mspec(2),
                  pl.BlockSpec((1, D), lambda i: (0, 0)),
                  mspec(4), mspec(3),
                  pl.BlockSpec((D, 2 * LANES), lambda i: (0, 0)),
                  pl.BlockSpec((N_EXPERTS, 1), lambda i: (0, 0))],
        out_specs=[pl.BlockSpec((tm, D), lambda i: (i, 0)),
                   pl.BlockSpec((tm, D), lambda i: (i, 0)),
                   pl.BlockSpec((2, tm), lambda i: (0, i)),
                   pl.BlockSpec((2, tm), lambda i: (0, i)),
                   pl.BlockSpec((tm, 8), lambda i: (i, 0)),
                   pl.BlockSpec((N_EXPERTS, LANES), lambda i: (0, 0))],
        out_shape=[jax.ShapeDtypeStruct((N, D), F32),
                   jax.ShapeDtypeStruct((N, D), F32),
                   jax.ShapeDtypeStruct((2, N), I32),
                   jax.ShapeDtypeStruct((2, N), I32),
                   jax.ShapeDtypeStruct((N, 8), F32),
                   jax.ShapeDtypeStruct((N_EXPERTS, LANES), F32)],
        scratch_shapes=[pltpu.VMEM((N_EXPERTS, LANES), F32), pltpu.VMEM((tm, tm), BF16)],
        compiler_params=_params(("arbitrary",)),
        name="outproj_router",
    )(merged, w_out, x2, mod, norm_g.reshape(1, D), mod, mod, wr, rb)


DMA_UNROLL = 8


def _dispatch_kernel(zb_ref, dest_ref, h_ref, xs_ref, zero_ref, sem, zsem):
    tm = h_ref.shape[0]

    @pl.when(pl.program_id(0) == 0)
    def _():
        zero_ref[...] = jnp.zeros_like(zero_ref)

        def zero_copy(n):
            start = pl.multiple_of(zb_ref[n] * MOE_ROWS, MOE_ROWS)
            return pltpu.make_async_copy(zero_ref, xs_ref.at[pl.ds(start, MOE_ROWS), :], zsem)

        for n in range(2 * N_EXPERTS):
            @pl.when(zb_ref[n] >= 0)
            def _():
                zero_copy(n).start()
        for n in range(2 * N_EXPERTS):
            @pl.when(zb_ref[n] >= 0)
            def _():
                zero_copy(n).wait()

    def row_copy(t, k):
        return pltpu.make_async_copy(h_ref.at[pl.ds(t, 1), :], xs_ref.at[pl.ds(dest_ref[k * tm + t], 1), :], sem)

    def issue(t, carry):
        row_copy(t, 0).start()
        row_copy(t, 1).start()
        return carry

    def drain(t, carry):
        row_copy(t, 0).wait()
        row_copy(t, 1).wait()
        return carry

    lax.fori_loop(0, tm, issue, 0, unroll=DMA_UNROLL)
    lax.fori_loop(0, tm, drain, 0, unroll=DMA_UNROLL)


def _row_tile(T):
    return min(512, T)


def _tile_dest(dest, T):
    tm = _row_tile(T)
    return dest.reshape(2, -1, tm).transpose(1, 0, 2).reshape(-1)


def _dispatch(zero_blocks, dest, h, n_rows, T):
    N, D = h.shape
    tm = _row_tile(T)
    return pl.pallas_call(
        _dispatch_kernel,
        grid_spec=pltpu.PrefetchScalarGridSpec(
            num_scalar_prefetch=1,
            grid=(N // tm,),
            in_specs=[pl.BlockSpec((2 * tm,), lambda i, zb: (i,), memory_space=pltpu.SMEM),
                      pl.BlockSpec((tm, D), lambda i, zb: (i, 0))],
            out_specs=pl.BlockSpec(memory_space=pl.ANY),
            scratch_shapes=[pltpu.VMEM((MOE_ROWS, D), F32), pltpu.SemaphoreType.DMA(()),
                            pltpu.SemaphoreType.DMA(())]),
        out_shape=jax.ShapeDtypeStruct((n_rows, D), F32),
        compiler_params=_params(("arbitrary",)),
        name="moe_dispatch",
    )(zero_blocks, dest, h)


def _experts_kernel(be_ref, nb_ref, x_ref, wg_ref, wu_ref, wd_ref, y_ref):
    @pl.when(pl.program_id(0) < nb_ref[0])
    def _():
        x = x_ref[...].astype(BF16)
        a = jnp.dot(x, wg_ref[0, 0], preferred_element_type=F32)
        b = jnp.dot(x, wu_ref[0, 0], preferred_element_type=F32)
        hmid = (a * jax.nn.sigmoid(a) * b).astype(BF16)
        y_ref[...] = jnp.dot(hmid, wd_ref[0, 0], preferred_element_type=F32)

    @pl.when(pl.program_id(0) >= nb_ref[0])
    def _():
        y_ref[...] = jnp.zeros_like(y_ref)


def _experts(blk_e, n_used, xs, wg, wu, wd, l):
    R, D = xs.shape
    F = wg.shape[3]
    nb = R // MOE_ROWS
    row = lambda i, be, nu: (jnp.minimum(i, nu[0] - 1), 0)
    return pl.pallas_call(
        _experts_kernel,
        grid_spec=pltpu.PrefetchScalarGridSpec(
            num_scalar_prefetch=2,
            grid=(nb,),
            in_specs=[pl.BlockSpec((MOE_ROWS, D), row),
                      pl.BlockSpec((1, 1, D, F), lambda i, be, nu: (l, be[i], 0, 0)),
                      pl.BlockSpec((1, 1, D, F), lambda i, be, nu: (l, be[i], 0, 0)),
                      pl.BlockSpec((1, 1, F, D), lambda i, be, nu: (l, be[i], 0, 0))],
            out_specs=pl.BlockSpec((MOE_ROWS, D), lambda i, be, nu: (i, 0))),
        out_shape=jax.ShapeDtypeStruct((R, D), F32),
        compiler_params=_params(("arbitrary",)),
        name="moe_experts",
    )(blk_e, n_used, xs, wg, wu, wd)


def _combine_kernel(dest_ref, y_ref, wt_ref, x_ref, g2_ref, fg_ref, o_ref, buf_ref, sem, *, final):
    tm = x_ref.shape[0]
    parts = 2
    rows = tm // parts

    def issue(p):
        def body(t, carry):
            for k in range(2):
                pltpu.make_async_copy(y_ref.at[pl.ds(dest_ref[k * tm + t], 1), :],
                                      buf_ref.at[k, pl.ds(t, 1), :], sem.at[p]).start()
            return carry
        lax.fori_loop(p * rows, (p + 1) * rows, body, 0, unroll=DMA_UNROLL)

    def drain(p):
        def body(t, carry):
            for k in range(2):
                pltpu.make_async_copy(y_ref.at[pl.ds(dest_ref[k * tm + t], 1), :],
                                      buf_ref.at[k, pl.ds(t, 1), :], sem.at[p]).wait()
            return carry
        lax.fori_loop(p * rows, (p + 1) * rows, body, 0, unroll=DMA_UNROLL)

    for p in range(parts):
        issue(p)
    for p in range(parts):
        drain(p)
        sl = pl.ds(p * rows, rows)
        wt = wt_ref[sl, :]
        moe = buf_ref[0, sl, :] * wt[:, 0:1] + buf_ref[1, sl, :] * wt[:, 1:2]
        x = x_ref[sl, :] + g2_ref[0] * moe
        if final:
            x = x * lax.rsqrt(jnp.mean(x * x, axis=-1, keepdims=True) + EPS) * fg_ref[...]
        o_ref[sl, :] = x


def _combine(dest, y, wt, x2, mod, final_g, T, final):
    N, D = x2.shape
    tm = _row_tile(T)
    tpb = T // tm
    return pl.pallas_call(
        functools.partial(_combine_kernel, final=final),
        grid=(N // tm,),
        in_specs=[pl.BlockSpec((2 * tm,), lambda i: (i,), memory_space=pltpu.SMEM),
                  pl.BlockSpec(memory_space=pl.ANY),
                  pl.BlockSpec((tm, 8), lambda i: (i, 0)),
                  pl.BlockSpec((tm, D), lambda i: (i, 0)),
                  pl.BlockSpec((1, 1, D), lambda i: ((i // tpb) * N_MOD + 5, 0, 0)),
                  pl.BlockSpec((1, D), lambda i: (0, 0))],
        out_specs=pl.BlockSpec((tm, D), lambda i: (i, 0)),
        out_shape=jax.ShapeDtypeStruct((N, D), F32),
        scratch_shapes=[pltpu.VMEM((2, tm, D), F32), pltpu.SemaphoreType.DMA((2,))],
        compiler_params=_params(("arbitrary",)),
        name="moe_combine",
    )(dest, y, wt, x2, mod, final_g.reshape(1, D))


def _pack_w_in(w):
    L, D, _ = w.shape
    w = w.astype(BF16)
    widths = (GM_WIDTH, GM_WIDTH, 3 * DN_WIDTH, DN_WIDTH, DN_HEADS, DN_HEADS, ML_RANK, ML_RANK, ML_ROPE,
              3 * D)
    offs = [0]
    for wd in widths:
        offs.append(offs[-1] + wd)
    u, v, qkv, z, b, a, cq, ckv, kr, gates = (w[:, :, offs[i]:offs[i + 1]] for i in range(len(widths)))
    kr_rot = jnp.concatenate([-kr[:, :, ML_ROPE // 2:], kr[:, :, :ML_ROPE // 2]], axis=2)
    zeros = jnp.zeros((L, D, PROJ_COLS - OFF_BA - 2 * DN_HEADS), BF16)
    return jnp.concatenate([u, v, qkv, z, gates, cq, ckv, kr, kr_rot, b, a, zeros], axis=2)


def _block_table(counts, n_blocks):
    per = (counts + MOE_ROWS - 1) // MOE_ROWS
    ends = jnp.cumsum(per)
    first_row = (ends - per) * MOE_ROWS
    ends_le = ends[None, :] <= jnp.arange(n_blocks, dtype=I32)[:, None]
    blk_e = jnp.minimum(jnp.sum(ends_le.astype(I32), axis=1), N_EXPERTS - 1)
    n_used = ends[-1]
    tail = n_used + jnp.arange(N_EXPERTS, dtype=I32)
    zero_blocks = jnp.concatenate([jnp.where(per > 0, ends - 1, -1), jnp.where(tail < n_blocks, tail, -1)])
    return blk_e, n_used.reshape(1).astype(I32), first_row.astype(I32), zero_blocks.astype(I32)


def kernel(x, c, positions, w_ada, b_ada, norm1_g, norm2_g, w_in, gmlp_ln_g, gmlp_ln_b, gmlp_ws, gmlp_bs, dn_conv_w, dn_a_log, dn_dt_bias, dn_norm_g, mla_cq_g, mla_ckv_g, mla_w_uq, mla_w_ukv, w_br_gmlp, w_br_dn, w_br_mla, w_out, w_router, router_bias, w_gate, w_up, w_down, final_norm_g):
    B, T, D = x.shape
    L = w_in.shape[0]
    N = B * T
    x2 = x.reshape(N, D)
    pos_col = positions.reshape(N, 1)
    mod_all = _ada_mod(c, w_ada, b_ada).reshape(L, B * N_MOD, 1, D)

    perm = jnp.array([(r % N_GROUPS) * PER_GROUP + r // N_GROUPS for r in range(N_EXPERTS)], I32)
    wr = jnp.concatenate([w_router[:, perm], jnp.zeros((D, LANES - N_EXPERTS), F32)], axis=1)
    wr_hi = wr.astype(BF16)
    wr_hl = jnp.concatenate([wr_hi, (wr - wr_hi.astype(F32)).astype(BF16)], axis=1)
    rb = router_bias[perm].reshape(N_EXPERTS, 1)
    n_blocks = (2 * N) // MOE_ROWS + N_EXPERTS

    w_in_p = _pack_w_in(w_in)
    wq_p, wkv_p = _pack_mla_weights(mla_w_uq, mla_w_ukv)
    w_a, w_b, w_c, w_o = (w.astype(BF16) for w in (w_br_gmlp, w_br_dn, w_br_mla, w_out))
    w_g, w_u, w_d = (w.astype(BF16) for w in (w_gate, w_up, w_down))
    rope_cs = _rope_tables(pos_col, T)

    for l in range(L):
        mod = mod_all[l]
        proj = _inproj(x2, norm1_g[l], mod, w_in_p, l, T)
        ya = _gmlp(proj, gmlp_ln_g[l], gmlp_ln_b[l], gmlp_ws[l], gmlp_bs[l], T)
        yb = _deltanet(proj, dn_conv_w[l], dn_a_log[l], dn_dt_bias[l], dn_norm_g[l], B, T)
        q, k, v = _mla_proj(proj, rope_cs, mla_cq_g[l], mla_ckv_g[l], wq_p, wkv_p, l, B, T)
        yc = _flash(q, k, v).reshape(N, ML_HEADS * ML_V)
        merged = _merge(ya, yb, yc, proj, w_a, w_b, w_c, l, T)
        x2, h2, e, rank, wt, counts = _outproj(merged, w_o, l, x2, mod, norm2_g[l], wr_hl, rb, T)
        blk_e, n_used, first_row, zero_blocks = _block_table(counts[:, 0].astype(I32), n_blocks)
        onehot = e[:, :, None] == jnp.arange(N_EXPERTS, dtype=I32)
        dest = _tile_dest(rank + jnp.sum(jnp.where(onehot, first_row, 0), axis=-1), T)
        xs = _dispatch(zero_blocks, dest, h2, n_blocks * MOE_ROWS, T)
        y = _experts(blk_e, n_used, xs, w_g, w_u, w_d, l)
        x2 = _combine(dest, y, wt, x2, mod, final_norm_g, T, final=(l == L - 1))
    return x2.reshape(B, T, D)
```
